```python
import jax
import jax.numpy as jnp
from jax import lax
import numpy as np

D_MODEL = 1024
BATCH = 16
SEQ = 4096
DEPTH = 1
DEC_BATCH = 128
DEC_SEQ = 1
PAST_LEN = 8192
PAGE_SIZE = 128

MIX_WIDTH = D_MODEL
NSA_HEADS = 8
NSA_HEAD_DIM = 64
NSA_KV_GROUPS = 2
NSA_REP = NSA_HEADS // NSA_KV_GROUPS
NSA_WIDTH = NSA_HEADS * NSA_HEAD_DIM
CMP_BLOCK = 32
CMP_STRIDE = 16
CMP_RATIO = CMP_BLOCK // CMP_STRIDE
SLC_BLOCK = 64
SLC_TOPN = 16
WINDOW = 512
Q_BLOCK = 64
HGRN_HEADS = 4
HGRN_DK = 128
HGRN_DV = 128
HGRN_WIDTH = HGRN_HEADS * HGRN_DV
HGRN_CHUNK = 16
N_GROUPS = 4
EXPERTS_PER_GROUP = 4
N_EXPERTS = N_GROUPS * EXPERTS_PER_GROUP
EXPERT_TOPK = 2
D_EXPERT = 256
NORM_EPS = 1e-6
KV_COLS = NSA_KV_GROUPS * 2 * NSA_HEAD_DIM
NSA_GATE_COLS = 3 * NSA_HEADS
HGRN_QK_COLS = HGRN_HEADS * HGRN_DK
HGRN_V_COLS = HGRN_HEADS * HGRN_DV
IN_COLS = NSA_WIDTH + 3 * KV_COLS + NSA_GATE_COLS + 2 * HGRN_QK_COLS + 2 * HGRN_V_COLS

kernel_name = 'hymba_nsa_hgrn2_hmoe_step'


def rms_norm(x, g):
    xf = x.astype(jnp.float32)
    y = xf * lax.rsqrt(jnp.mean(xf * xf, axis=-1, keepdims=True) + NORM_EPS)
    return (y * g.astype(jnp.float32)).astype(x.dtype)


def alibi_slopes():
    h = jnp.arange(1, NSA_HEADS + 1, dtype=jnp.float32)
    return jnp.exp2(-8.0 * h / NSA_HEADS).reshape(NSA_KV_GROUPS, NSA_REP)


def masked_softmax(logits, mask):
    logits = jnp.where(mask, logits, -jnp.inf)
    m = jnp.max(logits, axis=-1, keepdims=True)
    m = jnp.where(jnp.isfinite(m), m, 0.0)
    e = jnp.exp(logits - m)
    return e / jnp.maximum(jnp.sum(e, axis=-1, keepdims=True), 1e-30)


def adaln(c, w_ada, b_ada):
    a = jax.nn.silu(c) @ w_ada + b_ada
    return jnp.split(a[:, None, :], 6, axis=-1)


def split_projection(proj):
    b, t = proj.shape[0], proj.shape[1]
    sizes = [NSA_WIDTH, KV_COLS, KV_COLS, KV_COLS, NSA_GATE_COLS, HGRN_QK_COLS, HGRN_QK_COLS, HGRN_V_COLS, HGRN_V_COLS]
    parts = jnp.split(proj, np.cumsum(sizes)[:-1].tolist(), axis=-1)
    q = parts[0].reshape(b, t, NSA_KV_GROUPS, NSA_REP, NSA_HEAD_DIM) * (NSA_HEAD_DIM ** -0.5)
    kv_c = parts[1].reshape(b, t, NSA_KV_GROUPS, 2, NSA_HEAD_DIM)
    kv_s = parts[2].reshape(b, t, NSA_KV_GROUPS, 2, NSA_HEAD_DIM)
    kv_w = parts[3].reshape(b, t, NSA_KV_GROUPS, 2, NSA_HEAD_DIM)
    gates = parts[4].reshape(b, t, NSA_HEADS, 3)
    return (q, kv_c, kv_s, kv_w, gates, parts[5], parts[6], parts[7], parts[8])


def compress_kv(kv, w1, b1, w2):
    b, L = kv.shape[0], kv.shape[1]
    n_ch = L // CMP_STRIDE
    n_cmp = n_ch - CMP_RATIO + 1
    ch = kv[:, :n_ch * CMP_STRIDE].reshape(b, n_ch, CMP_STRIDE, NSA_KV_GROUPS, 2, NSA_HEAD_DIM)
    w1r = w1.reshape(2, CMP_RATIO, CMP_STRIDE, NSA_HEAD_DIM, NSA_HEAD_DIM)
    hid = b1[None, None, None]
    for r in range(CMP_RATIO):
        hid = hid + jnp.einsum('bnsgkd,ksde->bngke', ch[:, r:r + n_cmp], w1r[:, r])
    return jnp.einsum('bngke,kef->bngkf', jax.nn.gelu(hid), w2)


def cmp_branch(q, q_pos, kvc, slopes):
    n_cmp = kvc.shape[1]
    e_pos = jnp.arange(n_cmp) * CMP_STRIDE + (CMP_BLOCK - 1)
    dist = (q_pos[:, None] - e_pos[None, :]).astype(jnp.float32)
    logits = jnp.einsum('bqgrd,bngd->bqgrn', q, kvc[..., 0, :]).astype(jnp.float32)
    logits = logits - slopes[None, None, :, :, None] * dist[None, :, None, None, :]
    p = masked_softmax(logits, (dist >= 0)[None, :, None, None, :])
    o = jnp.einsum('bqgrn,bngd->bqgrd', p.astype(kvc.dtype), kvc[..., 1, :])
    return o, jnp.sum(p, axis=3)


def select_positions(p_grp, q_pos, seq_len):
    n_cmp = p_grp.shape[-1]
    n_slc = -(-seq_len // SLC_BLOCK)
    start = jnp.arange(n_cmp)[:, None] * CMP_STRIDE
    j = jnp.arange(n_slc)[None, :]
    overlap = ((start < (j + 1) * SLC_BLOCK) & (start + CMP_BLOCK > j * SLC_BLOCK)).astype(jnp.float32)
    p_slc = jnp.einsum('bqgn,nj->bqgj', p_grp, overlap)
    cur = (q_pos // SLC_BLOCK)[:, None]
    forced = (j == 0) | (j == cur) | (j == cur - 1)
    valid = j * SLC_BLOCK <= q_pos[:, None]
    score = jnp.where(forced[None, :, None, :], jnp.inf, jnp.where(valid[None, :, None, :], p_slc, -jnp.inf))
    k = min(SLC_TOPN, n_slc)
    _, idx = lax.top_k(score, k)
    pos = idx[..., None] * SLC_BLOCK + jnp.arange(SLC_BLOCK)
    return pos.reshape(idx.shape[0], idx.shape[1], idx.shape[2], k * SLC_BLOCK)


def slc_branch(q, q_pos, rows, pos, slopes):
    dist = (q_pos[None, :, None, None] - pos).astype(jnp.float32)
    logits = jnp.einsum('bqgrd,bqgkd->bqgrk', q, rows[..., 0, :]).astype(jnp.float32)
    logits = logits - slopes[None, None, :, :, None] * dist[:, :, :, None, :]
    p = masked_softmax(logits, (dist >= 0)[:, :, :, None, :])
    return jnp.einsum('bqgrk,bqgkd->bqgrd', p.astype(rows.dtype), rows[..., 1, :])


def win_branch(q, q_pos, kvw, k_pos, slopes):
    dist = q_pos[:, None] - k_pos[None, :]
    mask = (dist >= 0) & (dist < WINDOW) & (k_pos[None, :] >= 0)
    logits = jnp.einsum('bqgrd,bsgd->bqgrs', q, kvw[..., 0, :]).astype(jnp.float32)
    logits = logits - slopes[None, None, :, :, None] * dist.astype(jnp.float32)[None, :, None, None, :]
    p = masked_softmax(logits, mask[None, :, None, None, :])
    return jnp.einsum('bqgrs,bsgd->bqgrd', p.astype(kvw.dtype), kvw[..., 1, :])


def nsa_combine(gates, o_c, o_s, o_w):
    b, tq = gates.shape[0], gates.shape[1]
    g = jax.nn.sigmoid(gates.astype(jnp.float32)).reshape(b, tq, NSA_KV_GROUPS, NSA_REP, 3)
    o = g[..., 0:1] * o_c + g[..., 1:2] * o_s + g[..., 2:3] * o_w
    return o.reshape(b, tq, NSA_WIDTH).astype(o_c.dtype)


def nsa_prompt(q, kv_c, kv_s, kv_w, gates, w_phi1, b_phi1, w_phi2):
    b, t = q.shape[0], q.shape[1]
    slopes = alibi_slopes()
    kvc = compress_kv(kv_c, w_phi1, b_phi1, w_phi2)
    kvw_pad = jnp.pad(kv_w, ((0, 0), (WINDOW, 0), (0, 0), (0, 0), (0, 0)))
    b_idx = jnp.arange(b)[:, None, None, None]
    g_idx = jnp.arange(NSA_KV_GROUPS)[None, None, :, None]

    def block(q0):
        qb = lax.dynamic_slice_in_dim(q, q0, Q_BLOCK, axis=1)
        gb = lax.dynamic_slice_in_dim(gates, q0, Q_BLOCK, axis=1)
        q_pos = q0 + jnp.arange(Q_BLOCK)
        o_c, p_grp = cmp_branch(qb, q_pos, kvc, slopes)
        pos = select_positions(p_grp, q_pos, t)
        rows = kv_s[b_idx, jnp.minimum(pos, t - 1), g_idx]
        o_s = slc_branch(qb, q_pos, rows, pos, slopes)
        kw = lax.dynamic_slice_in_dim(kvw_pad, q0, WINDOW + Q_BLOCK, axis=1)
        k_pos = q0 - WINDOW + jnp.arange(WINDOW + Q_BLOCK)
        o_w = win_branch(qb, q_pos, kw, k_pos, slopes)
        return nsa_combine(gb, o_c, o_s, o_w)

    out = lax.map(block, jnp.arange(0, t, Q_BLOCK))
    return out.transpose(1, 0, 2, 3).reshape(b, t, NSA_WIDTH)


def nsa_sample(q, kv_c, kv_s, kv_w, gates, pool_c, pool_s, win_buf, page_table, w_phi1, b_phi1, w_phi2):
    b, t = q.shape[0], q.shape[1]
    past = page_table.shape[1] * PAGE_SIZE
    slopes = alibi_slopes()
    q_pos = past + jnp.arange(t)
    b_idx = jnp.arange(b)[:, None, None, None]
    g_idx = jnp.arange(NSA_KV_GROUPS)[None, None, :, None]
    past_c = pool_c[page_table].reshape(b, past, NSA_KV_GROUPS, 2, NSA_HEAD_DIM)
    kvc = compress_kv(jnp.concatenate([past_c, kv_c.astype(past_c.dtype)], axis=1), w_phi1, b_phi1, w_phi2)
    o_c, p_grp = cmp_branch(q, q_pos, kvc, slopes)
    pos = select_positions(p_grp, q_pos, past + t)
    pc = jnp.minimum(pos, past - 1)
    phys = page_table[b_idx, pc // PAGE_SIZE]
    rows_past = pool_s[phys, pc % PAGE_SIZE, g_idx]
    rows_new = kv_s[b_idx, jnp.clip(pos - past, 0, t - 1), g_idx].astype(rows_past.dtype)
    rows = jnp.where((pos < past)[..., None, None], rows_past, rows_new)
    o_s = slc_branch(q, q_pos, rows, pos, slopes)
    w_len = win_buf.shape[1]
    kw = jnp.concatenate([win_buf, kv_w.astype(win_buf.dtype)], axis=1)
    k_pos = past - w_len + jnp.arange(w_len + t)
    o_w = win_branch(q, q_pos, kw, k_pos, slopes)
    return nsa_combine(gates, o_c, o_s, o_w), kw[:, t:]


def hgrn2_chunked(q, k, v, logf, s0):
    b, t = q.shape[0], q.shape[1]
    n_ch = -(-t // HGRN_CHUNK)
    pad = n_ch * HGRN_CHUNK - t

    def prep(a):
        a = jnp.pad(a, ((0, 0), (0, pad), (0, 0), (0, 0)))
        return a.reshape(b, n_ch, HGRN_CHUNK, a.shape[2], a.shape[3]).transpose(1, 0, 3, 2, 4)

    causal = jnp.tril(jnp.ones((HGRN_CHUNK, HGRN_CHUNK), dtype=bool))

    def step(s, inp):
        qc, kc, vc, lc = inp
        bcum = jnp.cumsum(lc, axis=2)
        b_last = bcum[:, :, -1:, :]
        qe = qc * jnp.exp(bcum)
        ke = kc * jnp.exp(-bcum)
        o = jnp.einsum('bhck,bhkv->bhcv', qe, s)
        a = jnp.where(causal, jnp.einsum('bhck,bhsk->bhcs', qe, ke), 0.0)
        o = o + jnp.einsum('bhcs,bhsv->bhcv', a, vc)
        s_new = jnp.exp(b_last[:, :, 0, :])[..., None] * s + jnp.einsum('bhsk,bhsv->bhkv', kc * jnp.exp(b_last - bcum), vc)
        return s_new, o

    s_fin, os = lax.scan(step, s0, (prep(q), prep(k), prep(v), prep(logf)))
    o = os.transpose(1, 0, 3, 2, 4).reshape(b, n_ch * HGRN_CHUNK, HGRN_HEADS, HGRN_DV)[:, :t]
    return o, s_fin


def hgrn2_mixer(q_raw, f_raw, i_raw, og_raw, lb, g_out, s0):
    b, t = q_raw.shape[0], q_raw.shape[1]
    shp_k = (b, t, HGRN_HEADS, HGRN_DK)
    q = jax.nn.silu(q_raw.astype(jnp.float32)).reshape(shp_k)
    fz = f_raw.astype(jnp.float32)
    logf = jnp.log(lb + (1.0 - lb) * jax.nn.sigmoid(fz)).reshape(shp_k)
    k = ((1.0 - lb) * jax.nn.sigmoid(-fz)).reshape(shp_k)
    v = i_raw.astype(jnp.float32).reshape(b, t, HGRN_HEADS, HGRN_DV)
    o, s_new = hgrn2_chunked(q, k, v, logf, s0)
    o = rms_norm(o, g_out) * jax.nn.silu(og_raw.astype(jnp.float32)).reshape(b, t, HGRN_HEADS, HGRN_DV)
    return o.reshape(b, t, HGRN_WIDTH).astype(q_raw.dtype), s_new


def hier_moe(h, w_rg, b_rg, w_re, b_re, w_gate, w_up, w_down):
    b, t, d = h.shape
    hf = h.reshape(b * t, d)
    pg = jax.nn.softmax((hf @ w_rg).astype(jnp.float32) + b_rg.astype(jnp.float32), axis=-1)
    g_star = jnp.argmax(pg, axis=-1)
    pg_top = jnp.max(pg, axis=-1)
    le = ((hf @ w_re).astype(jnp.float32) + b_re.astype(jnp.float32)).reshape(-1, N_GROUPS, EXPERTS_PER_GROUP)
    le_g = jnp.einsum('nge,ng->ne', le, jax.nn.one_hot(g_star, N_GROUPS, dtype=jnp.float32))
    top_v, top_i = lax.top_k(le_g, EXPERT_TOPK)
    w_top = jax.nn.softmax(top_v, axis=-1) * pg_top[:, None]
    comb = jnp.einsum('nk,nke->ne', w_top, jax.nn.one_hot(g_star[:, None] * EXPERTS_PER_GROUP + top_i, N_EXPERTS, dtype=jnp.float32))
    y = jnp.zeros((b * t, d), jnp.float32)
    for grp in range(N_GROUPS):
        sl = slice(grp * EXPERTS_PER_GROUP, (grp + 1) * EXPERTS_PER_GROUP)
        hg = jnp.einsum('nd,edf->nef', hf, w_gate[sl])
        hu = jnp.einsum('nd,edf->nef', hf, w_up[sl])
        act = (jax.nn.silu(hg) * hu) * comb[:, sl, None]
        y = y + jnp.einsum('nef,efd->nd', act, w_down[sl]).astype(jnp.float32)
    return y.reshape(b, t, d).astype(h.dtype)


def setup_inputs(seed: int = 0) -> dict:
    key = jax.random.key(seed)
    ks = jax.random.split(key, 32)

    def nrm(i, shape, scale):
        return scale * jax.random.normal(ks[i], shape, jnp.float32)

    n_pages = PAST_LEN // PAGE_SIZE
    n_used = DEC_BATCH * n_pages
    n_pool = n_used + max(1, n_used // 4)
    w_buf = min(WINDOW, PAST_LEN)
    page_table = jax.random.permutation(ks[8], n_pool)[:n_used].reshape(DEC_BATCH, n_pages).astype(jnp.int32)
    D = D_MODEL
    return {
        'x_prompt': nrm(0, (BATCH, SEQ, D), 1.0),
        'x_sample': nrm(1, (DEC_BATCH, DEC_SEQ, D), 1.0),
        'c_prompt': nrm(2, (BATCH, D), 1.0),
        'c_sample': nrm(3, (DEC_BATCH, D), 1.0),
        'cache_cmp_kv': nrm(4, (DEPTH, n_pool, PAGE_SIZE, NSA_KV_GROUPS, 2, NSA_HEAD_DIM), 1.0),
        'cache_slc_kv': nrm(5, (DEPTH, n_pool, PAGE_SIZE, NSA_KV_GROUPS, 2, NSA_HEAD_DIM), 1.0),
        'cache_win_kv': nrm(6, (DEPTH, DEC_BATCH, w_buf, NSA_KV_GROUPS, 2, NSA_HEAD_DIM), 1.0),
        'state_hgrn': nrm(7, (DEPTH, DEC_BATCH, HGRN_HEADS, HGRN_DK, HGRN_DV), 0.5),
        'page_table': page_table,
        'w_ada': nrm(9, (DEPTH, D, 6 * D), 0.5 * D ** -0.5),
        'b_ada': nrm(10, (DEPTH, 6 * D), 0.02),
        'g_pre_mix': 1.0 + nrm(11, (DEPTH, D), 0.05),
        'g_post_mix': 1.0 + nrm(12, (DEPTH, D), 0.05),
        'g_pre_ffn': 1.0 + nrm(13, (DEPTH, D), 0.05),
        'g_post_ffn': 1.0 + nrm(14, (DEPTH, D), 0.05),
        'w_in': nrm(15, (DEPTH, D, IN_COLS), D ** -0.5),
        'w_phi1': nrm(16, (DEPTH, 2, CMP_BLOCK, NSA_HEAD_DIM, NSA_HEAD_DIM), (CMP_BLOCK * NSA_HEAD_DIM) ** -0.5),
        'b_phi1': nrm(17, (DEPTH, 2, NSA_HEAD_DIM), 0.02),
        'w_phi2': nrm(18, (DEPTH, 2, NSA_HEAD_DIM, NSA_HEAD_DIM), NSA_HEAD_DIM ** -0.5),
        'g_nsa_out': 1.0 + nrm(19, (DEPTH, NSA_WIDTH), 0.05),
        'hgrn_lb_logits': nrm(20, (DEPTH + 1, HGRN_HEADS * HGRN_DK), 0.1),
        'g_hgrn_out': 1.0 + nrm(21, (DEPTH, HGRN_HEADS, HGRN_DV), 0.05),
        'w_out': nrm(22, (DEPTH, MIX_WIDTH, D), MIX_WIDTH ** -0.5),
        'w_route_group': nrm(23, (DEPTH, D, N_GROUPS), D ** -0.5),
        'b_route_group': nrm(24, (DEPTH, N_GROUPS), 0.01),
        'w_route_expert': nrm(25, (DEPTH, D, N_EXPERTS), D ** -0.5),
        'b_route_expert': nrm(26, (DEPTH, N_EXPERTS), 0.01),
        'w_exp_gate': nrm(27, (DEPTH, N_EXPERTS, D, D_EXPERT), D ** -0.5),
        'w_exp_up': nrm(28, (DEPTH, N_EXPERTS, D, D_EXPERT), D ** -0.5),
        'w_exp_down': nrm(29, (DEPTH, N_EXPERTS, D_EXPERT, D), D_EXPERT ** -0.5),
    }


def reference(x_prompt, x_sample, c_prompt, c_sample, cache_cmp_kv, cache_slc_kv, cache_win_kv, state_hgrn, page_table,
              w_ada, b_ada, g_pre_mix, g_post_mix, g_pre_ffn, g_post_ffn, w_in, w_phi1, b_phi1, w_phi2, g_nsa_out,
              hgrn_lb_logits, g_hgrn_out, w_out, w_route_group, b_route_group, w_route_expert, b_route_expert,
              w_exp_gate, w_exp_up, w_exp_down):
    lower_bounds = jnp.cumsum(jax.nn.softmax(hgrn_lb_logits.astype(jnp.float32), axis=0), axis=0)

    def run_layer(x, c, l, mixer):
        sh_a, sc_a, gt_a, sh_f, sc_f, gt_f = adaln(c, w_ada[l], b_ada[l])
        h = rms_norm(x, g_pre_mix[l]) * (1.0 + sc_a) + sh_a
        parts = split_projection(h @ w_in[l])
        o_nsa, o_hgrn, states = mixer(parts, l)
        mixed = jnp.concatenate([rms_norm(o_nsa, g_nsa_out[l]), o_hgrn], axis=-1) @ w_out[l]
        x = x + gt_a * rms_norm(mixed, g_post_mix[l])
        h = rms_norm(x, g_pre_ffn[l]) * (1.0 + sc_f) + sh_f
        ffn = hier_moe(h, w_route_group[l], b_route_group[l], w_route_expert[l], b_route_expert[l],
                       w_exp_gate[l], w_exp_up[l], w_exp_down[l])
        x = x + gt_f * rms_norm(ffn, g_post_ffn[l])
        return x, states

    def prompt_mixer(parts, l):
        q, kv_c, kv_s, kv_w, gates, hq, hf, hi, hog = parts
        o_nsa = nsa_prompt(q, kv_c, kv_s, kv_w, gates, w_phi1[l], b_phi1[l], w_phi2[l])
        s0 = jnp.zeros((q.shape[0], HGRN_HEADS, HGRN_DK, HGRN_DV), jnp.float32)
        o_h, s_new = hgrn2_mixer(hq, hf, hi, hog, lower_bounds[l], g_hgrn_out[l], s0)
        w_keep = min(WINDOW, kv_w.shape[1])
        return o_nsa, o_h, (kv_c, kv_s, kv_w[:, kv_w.shape[1] - w_keep:], s_new.astype(x_prompt.dtype))

    def sample_mixer(parts, l):
        q, kv_c, kv_s, kv_w, gates, hq, hf, hi, hog = parts
        o_nsa, new_buf = nsa_sample(q, kv_c, kv_s, kv_w, gates, cache_cmp_kv[l], cache_slc_kv[l], cache_win_kv[l],
                                    page_table, w_phi1[l], b_phi1[l], w_phi2[l])
        o_h, s_new = hgrn2_mixer(hq, hf, hi, hog, lower_bounds[l], g_hgrn_out[l], state_hgrn[l].astype(jnp.float32))
        return o_nsa, o_h, (kv_c, kv_s, new_buf, s_new.astype(state_hgrn.dtype))

    y_p = x_prompt
    y_s = x_sample
    p_states = []
    s_states = []
    for l in range(DEPTH):
        y_p, st_p = run_layer(y_p, c_prompt, l, prompt_mixer)
        p_states.append(st_p)
        y_s, st_s = run_layer(y_s, c_sample, l, sample_mixer)
        s_states.append(st_s)
    p_cmp = jnp.stack([s[0] for s in p_states])
    p_slc = jnp.stack([s[1] for s in p_states])
    p_win = jnp.stack([s[2] for s in p_states])
    p_hgrn = jnp.stack([s[3] for s in p_states])
    s_cmp = jnp.stack([s[0] for s in s_states])
    s_slc = jnp.stack([s[1] for s in s_states])
    s_win = jnp.stack([s[2] for s in s_states])
    s_hgrn = jnp.stack([s[3] for s in s_states])
    return (y_p, y_s, p_cmp, p_slc, p_win, p_hgrn, s_cmp, s_slc, s_win, s_hgrn)
```

```python
import functools

import numpy as np
import jax
import jax.numpy as jnp
from jax import lax
from jax.experimental import pallas as pl
from jax.experimental.pallas import tpu as pltpu

F32 = jnp.float32
BF16 = jnp.bfloat16
HIGHEST = lax.Precision.HIGHEST

NSA_HEADS = 8
HEAD_DIM = 64
KV_GROUPS = 2
REP = NSA_HEADS // KV_GROUPS
CMP_BLOCK = 32
CMP_STRIDE = 16
SLC_BLOCK = 64
SLC_TOPN = 16
WINDOW = 512
HGRN_HEADS = 4
HGRN_DK = 128
HGRN_DV = 128
HGRN_SUB = 16
N_GROUPS = 4
EXPERTS_PER_GROUP = 4
N_EXPERTS = N_GROUPS * EXPERTS_PER_GROUP
D_EXPERT = 256
NORM_EPS = 1e-6

LANES = 128
KV_COLS = KV_GROUPS * 2 * HEAD_DIM
NSA_WIDTH = NSA_HEADS * HEAD_DIM
HGRN_WIDTH = HGRN_HEADS * HGRN_DV
GATE_ROWS = 16
MASKED = -1e30
M_INIT = -0.5e30
VMEM_LIMIT = 56 * 1024 * 1024

NT_DIMS = (((1,), (1,)), ((), ()))
TN_DIMS = (((0,), (0,)), ((), ()))


def _cparams(*sem):
    return pltpu.CompilerParams(dimension_semantics=sem, vmem_limit_bytes=VMEM_LIMIT)


def _rms(x, g):
    return x * lax.rsqrt(jnp.mean(x * x, axis=-1, keepdims=True) + NORM_EPS) * g


def _silu(x):
    return x * jax.nn.sigmoid(x)


def _dot(a, b, precise):
    if precise:
        return jnp.dot(a, b, preferred_element_type=F32, precision=HIGHEST)
    return jnp.dot(a.astype(BF16), b.astype(BF16), preferred_element_type=F32)


def _dot_nt(a, b, precise=False):
    if precise:
        return lax.dot_general(a, b, NT_DIMS, preferred_element_type=F32, precision=HIGHEST)
    return lax.dot_general(a.astype(BF16), b.astype(BF16), NT_DIMS, preferred_element_type=F32)


def _adaln_kernel(c_ref, w_ref, b_ref, o_ref):
    o_ref[...] = _dot(_silu(c_ref[...]), w_ref[...], True) + b_ref[...]


def _adaln(c, w, b):
    n, d = c.shape
    cols = w.shape[1]
    return pl.pallas_call(
        _adaln_kernel,
        grid=(cols // d,),
        in_specs=[pl.BlockSpec((n, d), lambda j: (0, 0)),
                  pl.BlockSpec((d, d), lambda j: (0, j)),
                  pl.BlockSpec((1, d), lambda j: (0, j))],
        out_specs=pl.BlockSpec((n, d), lambda j: (0, j)),
        out_shape=jax.ShapeDtypeStruct((n, cols), F32),
        compiler_params=_cparams("arbitrary"),
        name="adaln",
    )(c, w, b.reshape(1, cols))


Q_COLS = NSA_HEADS * LANES
HG_COLS = 4 * HGRN_WIDTH
NAT_COLS = Q_COLS + 3 * KV_COLS + HG_COLS
TR_ROWS = 2 * KV_COLS + KV_GROUPS * GATE_ROWS


def _prep_w_in(w_in):
    d = w_in.shape[0]
    sizes = [NSA_WIDTH, KV_COLS, KV_COLS, KV_COLS, 3 * NSA_HEADS, HGRN_WIDTH, HGRN_WIDTH, HGRN_WIDTH, HGRN_WIDTH]
    offs = np.cumsum([0] + sizes)
    wq = w_in[:, offs[0]:offs[1]].reshape(d, NSA_HEADS, HEAD_DIM) * (HEAD_DIM ** -0.5)
    wq = jnp.pad(wq, ((0, 0), (0, 0), (0, LANES - HEAD_DIM))).reshape(d, Q_COLS)
    w_nat = jnp.concatenate([wq, w_in[:, offs[1]:offs[4]], w_in[:, offs[5]:offs[9]]], axis=1)
    wg = w_in[:, offs[4]:offs[5]].reshape(d, KV_GROUPS, REP, 3).transpose(0, 1, 3, 2)
    wg = jnp.pad(wg.reshape(d, KV_GROUPS, 3 * REP), ((0, 0), (0, 0), (0, GATE_ROWS - 3 * REP)))
    w_tr = jnp.concatenate([w_in[:, offs[2]:offs[3]], w_in[:, offs[3]:offs[4]],
                            wg.reshape(d, KV_GROUPS * GATE_ROWS)], axis=1).T
    return w_nat, w_tr


def _inproj_prompt_kernel(x_ref, sc_ref, sh_ref, g_ref, wn_ref, wt_ref,
                          q_ref, kvc_ref, kvs_ref, kvw_ref, ksb_ref, kwb_ref, hg_ref,
                          vst_ref, vwt_ref, gt_ref):
    h = _rms(x_ref[0], g_ref[...]) * (1.0 + sc_ref[0]) + sh_ref[0]
    hb = h.astype(BF16)

    def nat(lo, hi):
        return jnp.dot(hb, wn_ref[:, lo:hi], preferred_element_type=F32)

    q_ref[0] = nat(0, Q_COLS).astype(BF16)
    c = Q_COLS
    kvc_ref[0] = nat(c, c + KV_COLS)
    kvs = nat(c + KV_COLS, c + 2 * KV_COLS)
    kvs_ref[0] = kvs
    ksb_ref[0] = kvs.astype(BF16)
    kvw = nat(c + 2 * KV_COLS, c + 3 * KV_COLS)
    kvw_ref[0] = kvw
    kwb_ref[0] = kvw.astype(BF16)
    hg_ref[0] = nat(c + 3 * KV_COLS, c + 3 * KV_COLS + HG_COLS)

    tr = lax.dot_general(wt_ref[...], hb, NT_DIMS, preferred_element_type=F32)
    row = lax.broadcasted_iota(jnp.int32, (KV_COLS, 1), 0)
    is_k = (row % LANES) < HEAD_DIM
    vst_ref[0, 0] = jnp.where(is_k, 1.0, tr[0:KV_COLS]).astype(BF16)
    vwt_ref[0, 0] = jnp.where(is_k, 1.0, tr[KV_COLS:2 * KV_COLS]).astype(BF16)
    gt_ref[0] = jax.nn.sigmoid(tr[2 * KV_COLS:])


def _inproj_prompt(x, sc, sh, g, w_nat, w_tr, tm):
    b, t, d = x.shape
    nt = t // tm
    row = lambda bi, i: (bi, i, 0)
    const = lambda bi, i: (0, 0)
    out_shapes = [
        jax.ShapeDtypeStruct((b, t, Q_COLS), BF16),
        jax.ShapeDtypeStruct((b, t, KV_COLS), F32),
        jax.ShapeDtypeStruct((b, t, KV_COLS), F32),
        jax.ShapeDtypeStruct((b, t, KV_COLS), F32),
        jax.ShapeDtypeStruct((b, t, KV_COLS), BF16),
        jax.ShapeDtypeStruct((b, t, KV_COLS), BF16),
        jax.ShapeDtypeStruct((b, t, HG_COLS), F32),
        jax.ShapeDtypeStruct((b, nt, KV_COLS, tm), BF16),
        jax.ShapeDtypeStruct((b, nt, KV_COLS, tm), BF16),
        jax.ShapeDtypeStruct((b, KV_GROUPS * GATE_ROWS, t), F32),
    ]
    out_specs = [
        pl.BlockSpec((1, tm, Q_COLS), row),
        pl.BlockSpec((1, tm, KV_COLS), row),
        pl.BlockSpec((1, tm, KV_COLS), row),
        pl.BlockSpec((1, tm, KV_COLS), row),
        pl.BlockSpec((1, tm, KV_COLS), row),
        pl.BlockSpec((1, tm, KV_COLS), row),
        pl.BlockSpec((1, tm, HG_COLS), row),
        pl.BlockSpec((1, 1, KV_COLS, tm), lambda bi, i: (bi, i, 0, 0)),
        pl.BlockSpec((1, 1, KV_COLS, tm), lambda bi, i: (bi, i, 0, 0)),
        pl.BlockSpec((1, KV_GROUPS * GATE_ROWS, tm), lambda bi, i: (bi, 0, i)),
    ]
    return pl.pallas_call(
        _inproj_prompt_kernel,
        grid=(b, nt),
        in_specs=[pl.BlockSpec((1, tm, d), row),
                  pl.BlockSpec((1, 1, d), lambda bi, i: (bi, 0, 0)),
                  pl.BlockSpec((1, 1, d), lambda bi, i: (bi, 0, 0)),
                  pl.BlockSpec((1, d), const),
                  pl.BlockSpec((d, NAT_COLS), const),
                  pl.BlockSpec((TR_ROWS, d), const)],
        out_specs=out_specs,
        out_shape=out_shapes,
        compiler_params=_cparams("parallel", "parallel"),
        name="inproj_prompt",
    )(x, sc, sh, g, w_nat, w_tr)


def _prep_compress(w1, b1, w2):
    ratio = CMP_BLOCK // CMP_STRIDE
    w1r = w1.reshape(2, ratio, CMP_STRIDE, HEAD_DIM, HEAD_DIM)
    eye_k = jnp.eye(2, dtype=F32)
    w01 = jnp.einsum('krsde,kl->skdrle', w1r, eye_k).reshape(CMP_STRIDE, LANES, ratio * LANES)
    w2n = jnp.einsum('kef,kl->kelf', w2, eye_k).reshape(LANES, LANES)
    return w01.astype(BF16), b1.reshape(1, LANES), w2n.astype(BF16), w2n.T.astype(BF16)


def _gelu_tanh(x):
    return 0.5 * x * (1.0 + jnp.tanh(0.7978845608028654 * (x + 0.044715 * (x * x * x))))


def _compress_group(load_rows, n_chunks, w01_ref, b1_ref, w2n_ref, w2t_ref):
    y = jnp.zeros((n_chunks, 2 * LANES), F32)
    for s in range(CMP_STRIDE):
        y = y + jnp.dot(load_rows(s).astype(BF16), w01_ref[s], preferred_element_type=F32)
    nxt = pltpu.roll(y[:, LANES:], n_chunks - 1, 0)
    hb = _gelu_tanh(y[:, :LANES] + nxt + b1_ref[...]).astype(BF16)
    return (jnp.dot(hb, w2n_ref[...], preferred_element_type=F32),
            lax.dot_general(w2t_ref[...], hb, NT_DIMS, preferred_element_type=F32))


def _compress_prompt_kernel(*refs):
    x_refs = refs[:KV_GROUPS]
    w01_ref, b1_ref, w2n_ref, w2t_ref, kc_ref, kct_ref = refs[KV_GROUPS:]
    n_chunks = kc_ref.shape[1]
    for g in range(KV_GROUPS):
        load = lambda s, g=g: x_refs[g][0, pl.ds(s, n_chunks, stride=CMP_STRIDE), :]
        kc, kct = _compress_group(load, n_chunks, w01_ref, b1_ref, w2n_ref, w2t_ref)
        kc_ref[0, :, g * LANES:(g + 1) * LANES] = kc.astype(BF16)
        kct_ref[0, g * LANES:(g + 1) * LANES, :] = kct.astype(BF16)


def _compress_prompt(kv_c, w01, b1r, w2n, w2t):
    b, t, _ = kv_c.shape
    n_chunks = t // CMP_STRIDE
    const2 = lambda i: (0, 0)
    return pl.pallas_call(
        _compress_prompt_kernel,
        grid=(b,),
        in_specs=[pl.BlockSpec((1, t, LANES), lambda i, g=g: (i, 0, g)) for g in range(KV_GROUPS)]
        + [pl.BlockSpec(w01.shape, lambda i: (0, 0, 0)),
           pl.BlockSpec((1, LANES), const2),
           pl.BlockSpec((LANES, LANES), const2),
           pl.BlockSpec((LANES, LANES), const2)],
        out_specs=[pl.BlockSpec((1, n_chunks, KV_COLS), lambda i: (i, 0, 0)),
                   pl.BlockSpec((1, KV_COLS, n_chunks), lambda i: (i, 0, 0))],
        out_shape=[jax.ShapeDtypeStruct((b, n_chunks, KV_COLS), BF16),
                   jax.ShapeDtypeStruct((b, KV_COLS, n_chunks), BF16)],
        compiler_params=_cparams("parallel"),
        name="compress_prompt",
    )(*([kv_c] * KV_GROUPS), w01, b1r, w2n, w2t)


def _overlap_t(n_chunks, n_slc):
    start = np.arange(n_chunks)[None, :] * CMP_STRIDE
    j = np.arange(n_slc)[:, None]
    return ((start < (j + 1) * SLC_BLOCK) & (start + CMP_BLOCK > j * SLC_BLOCK)).astype(np.float32)


def _split3(x):
    hi = x.astype(BF16)
    r1 = x - hi.astype(F32)
    mid = r1.astype(BF16)
    lo = (r1 - mid.astype(F32)).astype(BF16)
    return hi, mid, lo


def _rank_select(score, topn):
    n = score.shape[0]
    j = lax.broadcasted_iota(jnp.int32, score.shape, 0)
    cnt = jnp.zeros(score.shape, F32)
    for jp in range(n):
        row = score[jp:jp + 1, :]
        beats = (row > score) | ((row == score) & (j > jp))
        cnt = cnt + beats.astype(F32)
    return (cnt < topn).astype(F32)


def _nsa_prompt_kernel(slopes_ref, q_ref, kc_ref, kct_ref, ks_ref, vst_ref, kw_ref, vwt_ref, gt_ref, ovt_ref,
                       o_ref, sel_ref, m_ref, acc_ref):
    g = pl.program_id(1)
    i = pl.program_id(2)
    tq = q_ref.shape[1]
    kt = vst_ref.shape[3]
    n_chunks = kc_ref.shape[1]
    n_slc = ovt_ref.shape[0]
    blocks_per_tile = kt // SLC_BLOCK
    q0 = i * tq
    qpos = q0 + lax.broadcasted_iota(jnp.int32, (1, tq), 1)
    qpos_f = qpos.astype(F32)
    slopes = [slopes_ref[g * REP + r] for r in range(REP)]
    qh = [q_ref[0, :, r * LANES:(r + 1) * LANES] for r in range(REP)]

    e_pos = (lax.broadcasted_iota(jnp.int32, (n_chunks, 1), 0) * CMP_STRIDE + (CMP_BLOCK - 1)).astype(F32)
    dist_c = qpos_f - e_pos
    valid_c = dist_c >= 0.0
    kc = kc_ref[0]
    kct = kct_ref[0]
    p_grp = jnp.zeros((n_chunks, tq), F32)
    o_cmp = []
    for r in range(REP):
        s = lax.dot_general(kc, qh[r], NT_DIMS, preferred_element_type=F32) - slopes[r] * dist_c
        s = jnp.where(valid_c, s, MASKED)
        m = jnp.max(s, axis=0, keepdims=True)
        m = jnp.where(m > M_INIT, m, 0.0)
        e = jnp.exp(s - m)
        p = e / jnp.maximum(jnp.sum(e, axis=0, keepdims=True), 1e-30)
        p_grp = p_grp + p
        o_cmp.append(jnp.dot(kct, p.astype(BF16), preferred_element_type=F32)[HEAD_DIM:])
    ovt = ovt_ref[...]
    p_slc = sum(jnp.dot(ovt, part, preferred_element_type=F32) for part in _split3(p_grp))
    j = lax.broadcasted_iota(jnp.int32, (n_slc, 1), 0)
    cur = qpos // SLC_BLOCK
    forced = (j == 0) | (j == cur) | (j == cur - 1)
    valid_j = (j * SLC_BLOCK) <= qpos
    score = jnp.where(forced, jnp.inf, jnp.where(valid_j, p_slc, -jnp.inf))
    sel_ref[...] = _rank_select(score, min(SLC_TOPN, n_slc))

    delta0 = (lax.broadcasted_iota(jnp.int32, (kt, tq), 0) - lax.broadcasted_iota(jnp.int32, (kt, tq), 1)).astype(F32)

    def attend(k_ref, vt_ref, tile, valid_fn):
        k0 = pl.multiple_of(tile * kt, kt)
        keys = k_ref[0, pl.ds(k0, kt), :]
        vals = vt_ref[0, tile]
        delta = delta0 + (k0 - q0).astype(F32)
        valid = valid_fn(tile, delta)
        for r in range(REP):
            s = lax.dot_general(keys, qh[r], NT_DIMS, preferred_element_type=F32) + slopes[r] * delta
            s = jnp.where(valid, s, MASKED)
            m_old = m_ref[r:r + 1, :]
            m_new = jnp.maximum(m_old, jnp.max(s, axis=0, keepdims=True))
            p = jnp.exp(s - m_new)
            acc_ref[r] = acc_ref[r] * jnp.exp(m_old - m_new) + jnp.dot(vals, p.astype(BF16), preferred_element_type=F32)
            m_ref[r:r + 1, :] = m_new

    def reset():
        m_ref[...] = jnp.full(m_ref.shape, M_INIT, F32)
        acc_ref[...] = jnp.zeros(acc_ref.shape, F32)

    def finish():
        return [acc_ref[r, HEAD_DIM:, :] / jnp.maximum(acc_ref[r, 0:1, :], 1e-30) for r in range(REP)]

    def valid_slc(tile, delta):
        rows = [jnp.broadcast_to(sel_ref[pl.ds(tile * blocks_per_tile + b, 1), :], (SLC_BLOCK, tq))
                for b in range(blocks_per_tile)]
        return (jnp.concatenate(rows, axis=0) > 0.5) & (delta <= 0.0)

    reset()
    lax.fori_loop(0, (q0 + tq + kt - 1) // kt, lambda t_, c: (attend(ks_ref, vst_ref, t_, valid_slc), c)[1], 0)
    o_slc = finish()

    def valid_win(tile, delta):
        return (delta <= 0.0) & (delta > -float(WINDOW))

    reset()
    first = jnp.maximum(q0 - WINDOW, 0) // kt
    lax.fori_loop(first, (q0 + tq + kt - 1) // kt, lambda t_, c: (attend(kw_ref, vwt_ref, t_, valid_win), c)[1], 0)
    o_win = finish()

    gates = gt_ref[0]
    outs = []
    for r in range(REP):
        outs.append(gates[r:r + 1] * o_cmp[r] + gates[REP + r:REP + r + 1] * o_slc[r]
                    + gates[2 * REP + r:2 * REP + r + 1] * o_win[r])
    o_ref[0] = jnp.concatenate(outs, axis=0).T


def _nsa_prompt(q, kc, kct, ksb, vst, kwb, vwt, gates_t, tq):
    b, t, _ = q.shape
    n_chunks = kc.shape[1]
    n_tiles, kt = vst.shape[1], vst.shape[3]
    n_slc = -(-t // SLC_BLOCK)
    slopes = jnp.exp2(-8.0 * jnp.arange(1, NSA_HEADS + 1, dtype=F32) / NSA_HEADS)
    ovt = jnp.asarray(_overlap_t(n_chunks, n_slc), BF16)
    grp = lambda bi, gi, i, s: (bi, 0, gi)
    grid_spec = pltpu.PrefetchScalarGridSpec(
        num_scalar_prefetch=1,
        grid=(b, KV_GROUPS, t // tq),
        in_specs=[pl.BlockSpec((1, tq, REP * LANES), lambda bi, gi, i, s: (bi, i, gi)),
                  pl.BlockSpec((1, n_chunks, LANES), grp),
                  pl.BlockSpec((1, LANES, n_chunks), lambda bi, gi, i, s: (bi, gi, 0)),
                  pl.BlockSpec((1, t, LANES), grp),
                  pl.BlockSpec((1, n_tiles, LANES, kt), lambda bi, gi, i, s: (bi, 0, gi, 0)),
                  pl.BlockSpec((1, t, LANES), grp),
                  pl.BlockSpec((1, n_tiles, LANES, kt), lambda bi, gi, i, s: (bi, 0, gi, 0)),
                  pl.BlockSpec((1, GATE_ROWS, tq), lambda bi, gi, i, s: (bi, gi, i)),
                  pl.BlockSpec((n_slc, n_chunks), lambda bi, gi, i, s: (0, 0))],
        out_specs=pl.BlockSpec((1, tq, REP * HEAD_DIM), lambda bi, gi, i, s: (bi, i, gi)),
        scratch_shapes=[pltpu.VMEM((n_slc, tq), F32),
                        pltpu.VMEM((8, tq), F32),
                        pltpu.VMEM((REP, LANES, tq), F32)],
    )
    return pl.pallas_call(
        _nsa_prompt_kernel,
        grid_spec=grid_spec,
        out_shape=jax.ShapeDtypeStruct((b, t, NSA_WIDTH), F32),
        compiler_params=_cparams("parallel", "parallel", "arbitrary"),
        name="nsa_prompt",
    )(slopes, q, kc, kct, ksb, vst, kwb, vwt, gates_t, ovt)


def _lower_bound(lb_ref, layer):
    x = lb_ref[...]
    e = jnp.exp(x - jnp.max(x, axis=0, keepdims=True))
    return jnp.sum(e[0:layer + 1], axis=0, keepdims=True) / jnp.sum(e, axis=0, keepdims=True)


def _row_to_col(row):
    n = row.shape[1]
    eye = lax.broadcasted_iota(jnp.int32, (n, n), 0) == lax.broadcasted_iota(jnp.int32, (n, n), 1)
    return jnp.sum(jnp.where(eye, jnp.broadcast_to(row, (n, n)), 0.0), axis=1, keepdims=True)


def _cumsum_rows(x):
    n = x.shape[0]
    row = lax.broadcasted_iota(jnp.int32, (n, 1), 0)
    shift = 1
    while shift < n:
        x = x + jnp.where(row >= shift, pltpu.roll(x, shift, 0), 0.0)
        shift *= 2
    return x


def _hgrn_prompt_kernel(layer, q_ref, f_ref, i_ref, og_ref, lb_ref, g_ref, o_ref, s_ref, state):
    t_idx = pl.program_id(2)
    blk = HGRN_DK
    n_blk = q_ref.shape[1] // blk
    n_sub = blk // HGRN_SUB

    @pl.when(t_idx == 0)
    def _():
        state[...] = jnp.zeros(state.shape, F32)

    lb = _lower_bound(lb_ref, layer)
    gain = g_ref[0]
    row = lax.broadcasted_iota(jnp.int32, (blk, 1), 0)
    causal = lax.broadcasted_iota(jnp.int32, (blk, blk), 1) <= lax.broadcasted_iota(jnp.int32, (blk, blk), 0)

    def block(bi, carry):
        r0 = pl.multiple_of(bi * blk, blk)
        rows = pl.ds(r0, blk)
        fz = f_ref[0, rows, :]
        q = _silu(q_ref[0, rows, :])
        k = (1.0 - lb) * jax.nn.sigmoid(-fz)
        v = i_ref[0, rows, :].astype(BF16)
        cum = _cumsum_rows(jnp.log(lb + (1.0 - lb) * jax.nn.sigmoid(fz)))
        s0 = state[...]
        a_rows = []
        for sub in range(n_sub):
            lo, hi = sub * HGRN_SUB, (sub + 1) * HGRN_SUB
            ref_row = cum[lo - 1:lo] if sub else jnp.zeros((1, blk), F32)
            qe = q[lo:hi] * jnp.exp(cum[lo:hi] - ref_row)
            ke = k * jnp.exp(jnp.where(row < hi, ref_row - cum, MASKED))
            a_rows.append(_dot_nt(qe, ke))
        a = jnp.where(causal, jnp.concatenate(a_rows, axis=0), 0.0)
        o = (jnp.dot(a.astype(BF16), v, preferred_element_type=F32)
             + _dot(q * jnp.exp(cum), s0, False))
        last = cum[blk - 1:blk]
        kd = (k * jnp.exp(last - cum)).astype(BF16)
        state[...] = _row_to_col(jnp.exp(last)) * s0 + lax.dot_general(kd, v, TN_DIMS, preferred_element_type=F32)
        o_ref[0, rows, :] = (_rms(o, gain) * _silu(og_ref[0, rows, :])).astype(o_ref.dtype)
        return carry

    lax.fori_loop(0, n_blk, block, 0)

    @pl.when(t_idx == pl.num_programs(2) - 1)
    def _():
        s_ref[0, 0] = state[...]


def _hgrn_prompt(hg, lb_logits, g_out, layer, tc):
    b, t, _ = hg.shape
    part = lambda p: pl.BlockSpec((1, tc, HGRN_DK), lambda bi, h, i, p=p: (bi, i, p * HGRN_HEADS + h))
    n_layers = lb_logits.shape[0]
    return pl.pallas_call(
        functools.partial(_hgrn_prompt_kernel, layer),
        grid=(b, HGRN_HEADS, t // tc),
        in_specs=[part(0), part(1), part(2), part(3),
                  pl.BlockSpec((n_layers, HGRN_DK), lambda bi, h, i: (0, h)),
                  pl.BlockSpec((1, 1, HGRN_DV), lambda bi, h, i: (h, 0, 0))],
        out_specs=[pl.BlockSpec((1, tc, HGRN_DV), lambda bi, h, i: (bi, i, h)),
                   pl.BlockSpec((1, 1, HGRN_DK, HGRN_DV), lambda bi, h, i: (bi, h, 0, 0))],
        out_shape=[jax.ShapeDtypeStruct((b, t, HGRN_WIDTH), BF16),
                   jax.ShapeDtypeStruct((b, HGRN_HEADS, HGRN_DK, HGRN_DV), F32)],
        scratch_shapes=[pltpu.VMEM((HGRN_DK, HGRN_DV), F32)],
        compiler_params=_cparams("parallel", "parallel", "arbitrary"),
        name="hgrn_prompt",
    )(hg, hg, hg, hg, lb_logits, g_out.reshape(HGRN_HEADS, 1, HGRN_DV))


ROUTE_LANES = LANES
EXPERT_LANE0 = N_GROUPS


def _prep_router(w_rg, b_rg, w_re, b_re):
    d = w_rg.shape[0]
    pad = ROUTE_LANES - N_GROUPS - N_EXPERTS
    w = jnp.concatenate([w_rg, w_re, jnp.zeros((d, pad), F32)], axis=1)
    b = jnp.concatenate([b_rg, b_re, jnp.zeros((pad,), F32)]).reshape(1, ROUTE_LANES)
    return w, b


def _first_lane_of_max(x, lane):
    m = jnp.max(x, axis=1, keepdims=True)
    return m, jnp.min(jnp.where(x == m, lane, float(ROUTE_LANES)), axis=1, keepdims=True)


def _route(logits):
    lane = lax.broadcasted_iota(jnp.int32, (1, ROUTE_LANES), 1).astype(F32)
    is_grp = lane < N_GROUPS
    is_exp = (lane >= EXPERT_LANE0) & (lane < EXPERT_LANE0 + N_EXPERTS)
    lg = jnp.where(is_grp, logits, -jnp.inf)
    mg, g_star = _first_lane_of_max(lg, lane)
    pg_top = 1.0 / jnp.sum(jnp.exp(lg - mg), axis=1, keepdims=True)
    exp_grp = jnp.floor((lane - EXPERT_LANE0) / EXPERTS_PER_GROUP)
    le = jnp.where(is_exp & (exp_grp == g_star), logits, -jnp.inf)
    v1, i1 = _first_lane_of_max(le, lane)
    v2, i2 = _first_lane_of_max(jnp.where(lane == i1, -jnp.inf, le), lane)
    e2 = jnp.exp(v2 - v1)
    w1 = pg_top / (1.0 + e2)
    return jnp.where(lane == i1, w1, 0.0) + jnp.where(lane == i2, w1 * e2, 0.0)


def _postmix_kernel(precise, on_ref, oh_ref, x_ref, gt_ref, sc_ref, sh_ref, gn_ref, gp_ref, gf_ref,
                    wo_ref, wr_ref, br_ref, x1_ref, h2_ref, comb_ref):
    on = _rms(on_ref[0], gn_ref[...])
    mixed = _dot(on, wo_ref[0:NSA_WIDTH, :], precise) + _dot(oh_ref[0], wo_ref[NSA_WIDTH:, :], precise)
    x1 = x_ref[0] + gt_ref[0] * _rms(mixed, gp_ref[...])
    h2 = _rms(x1, gf_ref[...]) * (1.0 + sc_ref[0]) + sh_ref[0]
    x1_ref[0] = x1
    h2_ref[0] = h2.astype(h2_ref.dtype)
    comb_ref[0] = _route(_dot(h2, wr_ref[...], True) + br_ref[...])


def _postmix(o_nsa, o_h, x, gt, sc, sh, g_nsa, g_post, g_pre_ffn, w_out, w_r, b_r, tm, precise):
    b, t, d = x.shape
    row = lambda bi, i: (bi, i, 0)
    const = lambda bi, i: (0, 0)
    mod = (pl.BlockSpec((1, 1, d), lambda bi, i: (bi, 0, 0)) if gt.shape[1] == 1
           else pl.BlockSpec((1, tm, d), row))
    vec = lambda n: pl.BlockSpec((1, n), const)
    return pl.pallas_call(
        functools.partial(_postmix_kernel, precise),
        grid=(b, t // tm),
        in_specs=[pl.BlockSpec((1, tm, NSA_WIDTH), row), pl.BlockSpec((1, tm, HGRN_WIDTH), row),
                  pl.BlockSpec((1, tm, d), row), mod, mod, mod,
                  vec(NSA_WIDTH), vec(d), vec(d),
                  pl.BlockSpec(w_out.shape, const), pl.BlockSpec(w_r.shape, const), vec(ROUTE_LANES)],
        out_specs=[pl.BlockSpec((1, tm, d), row), pl.BlockSpec((1, tm, d), row),
                   pl.BlockSpec((1, tm, ROUTE_LANES), row)],
        out_shape=[jax.ShapeDtypeStruct((b, t, d), F32),
                   jax.ShapeDtypeStruct((b, t, d), F32 if precise else BF16),
                   jax.ShapeDtypeStruct((b, t, ROUTE_LANES), F32)],
        compiler_params=_cparams("parallel", "parallel"),
        name="postmix",
    )(o_nsa, o_h, x, gt, sc, sh, g_nsa.reshape(1, -1), g_post.reshape(1, -1), g_pre_ffn.reshape(1, -1),
      w_out, w_r, b_r)


def _prep_experts(w_gate, w_up, w_down, dtype):
    e, d, f = w_gate.shape
    side = lambda w: w.reshape(N_GROUPS, EXPERTS_PER_GROUP, d, f).transpose(0, 2, 1, 3).reshape(
        N_GROUPS, d, EXPERTS_PER_GROUP * f).astype(dtype)
    return side(w_gate), side(w_up), w_down.reshape(N_GROUPS, EXPERTS_PER_GROUP * f, d).astype(dtype)


def _moe_kernel(precise, h_ref, comb_ref, x1_ref, gt_ref, gp_ref, wg_ref, wu_ref, wd_ref, y_ref, acc_ref):
    grp = pl.program_id(2)

    @pl.when(grp == 0)
    def _():
        acc_ref[...] = jnp.zeros(acc_ref.shape, F32)

    h = h_ref[0]
    act = _silu(_dot(h, wg_ref[0], precise)) * _dot(h, wu_ref[0], precise)
    lane = lax.broadcasted_iota(jnp.int32, (1, ROUTE_LANES), 1)
    comb = comb_ref[0]
    parts = []
    for e in range(EXPERTS_PER_GROUP):
        sel = lane == EXPERT_LANE0 + grp * EXPERTS_PER_GROUP + e
        cw = jnp.sum(jnp.where(sel, comb, 0.0), axis=1, keepdims=True)
        parts.append(act[:, e * D_EXPERT:(e + 1) * D_EXPERT] * cw)
    acc_ref[...] += _dot(jnp.concatenate(parts, axis=1), wd_ref[0], precise)

    @pl.when(grp == pl.num_programs(2) - 1)
    def _():
        y_ref[0] = x1_ref[0] + gt_ref[0] * _rms(acc_ref[...], gp_ref[...])


def _moe(h2, comb, x1, gt, g_post, wg, wu, wd, tm, precise):
    b, t, d = x1.shape
    row = lambda bi, i, g: (bi, i, 0)
    mod = (pl.BlockSpec((1, 1, d), lambda bi, i, g: (bi, 0, 0)) if gt.shape[1] == 1
           else pl.BlockSpec((1, tm, d), row))
    wide = EXPERTS_PER_GROUP * D_EXPERT
    return pl.pallas_call(
        functools.partial(_moe_kernel, precise),
        grid=(b, t // tm, N_GROUPS),
        in_specs=[pl.BlockSpec((1, tm, d), row), pl.BlockSpec((1, tm, ROUTE_LANES), row),
                  pl.BlockSpec((1, tm, d), row), mod,
                  pl.BlockSpec((1, d), lambda bi, i, g: (0, 0)),
                  pl.BlockSpec((1, d, wide), lambda bi, i, g: (g, 0, 0)),
                  pl.BlockSpec((1, d, wide), lambda bi, i, g: (g, 0, 0)),
                  pl.BlockSpec((1, wide, d), lambda bi, i, g: (g, 0, 0))],
        out_specs=pl.BlockSpec((1, tm, d), row),
        out_shape=jax.ShapeDtypeStruct((b, t, d), F32),
        scratch_shapes=[pltpu.VMEM((tm, d), F32)],
        compiler_params=_cparams("parallel", "parallel", "arbitrary"),
        name="moe",
    )(h2, comb, x1, gt, g_post.reshape(1, -1), wg, wu, wd)


def _inproj_sample_kernel(x_ref, sc_ref, sh_ref, g_ref, w_ref, o_ref):
    h = _rms(x_ref[...], g_ref[...]) * (1.0 + sc_ref[...]) + sh_ref[...]
    o_ref[...] = _dot(h, w_ref[...], True)


def _inproj_sample(x, sc, sh, g, w):
    n, d = x.shape
    cols = w.shape[1]
    full = lambda shape: pl.BlockSpec(shape, lambda i: (0, 0))
    return pl.pallas_call(
        _inproj_sample_kernel,
        grid=(1,),
        in_specs=[full((n, d)), full((n, d)), full((n, d)), full((1, d)), full((d, cols))],
        out_specs=full((n, cols)),
        out_shape=jax.ShapeDtypeStruct((n, cols), F32),
        compiler_params=_cparams("arbitrary"),
        name="inproj_sample",
    )(x, sc, sh, g, w)


def _head_consts():
    row = lax.broadcasted_iota(jnp.int32, (NSA_HEADS, 1), 0)
    slope = jnp.exp2(-(row + 1).astype(F32) * (8.0 / NSA_HEADS))
    return row // REP, slope


def _masked_softmax_rows(s, slope, dist, valid):
    logits = jnp.where(valid, s - slope * dist, MASKED)
    m = jnp.max(logits, axis=1, keepdims=True)
    m = jnp.where(m > M_INIT, m, 0.0)
    e = jnp.exp(logits - m)
    return e / jnp.maximum(jnp.sum(e, axis=1, keepdims=True), 1e-30)


def _page_copies(pool_ref, pt_ref, x_buf, sem, b, slot, wait):
    n_pages = pt_ref.shape[1]
    page = pool_ref.shape[1]

    def body(p, carry):
        src_page = pt_ref[b, p]
        for g in range(KV_GROUPS):
            cp = pltpu.make_async_copy(pool_ref.at[src_page, :, pl.ds(g * LANES, LANES)],
                                       x_buf.at[slot, g, pl.ds(pl.multiple_of(p * page, page), page), :],
                                       sem.at[slot])
            if wait:
                cp.wait()
            else:
                cp.start()
        return carry

    lax.fori_loop(0, n_pages, body, 0)


def _nsa_sample_select_kernel(pt_ref, q_ref, pool_ref, w01_ref, b1_ref, w2n_ref, w2t_ref, ov_ref,
                              oc_ref, ids_ref, x_buf, sem):
    b = pl.program_id(0)
    slot = b % 2
    past = pt_ref.shape[1] * pool_ref.shape[1]
    n_chunks = past // CMP_STRIDE
    n_slc = -(-(past + 1) // SLC_BLOCK)
    nsp = ov_ref.shape[1]
    topn = ids_ref.shape[2]

    @pl.when(b == 0)
    def _():
        _page_copies(pool_ref, pt_ref, x_buf, sem, b, slot, False)

    @pl.when(b + 1 < pl.num_programs(0))
    def _():
        _page_copies(pool_ref, pt_ref, x_buf, sem, b + 1, 1 - slot, False)

    _page_copies(pool_ref, pt_ref, x_buf, sem, b, slot, True)

    q8 = q_ref[0]
    grp_of_row, slope = _head_consts()
    e_pos = lax.broadcasted_iota(jnp.int32, (1, n_chunks), 1) * CMP_STRIDE + (CMP_BLOCK - 1)
    dist = (past - e_pos).astype(F32)
    kcs = []
    s = jnp.zeros((NSA_HEADS, n_chunks), F32)
    for g in range(KV_GROUPS):
        load = lambda sub, g=g: x_buf[slot, g, pl.ds(sub, n_chunks, stride=CMP_STRIDE), :]
        kc, _ = _compress_group(load, n_chunks, w01_ref, b1_ref, w2n_ref, w2t_ref)
        kcs.append(kc)
        s = jnp.where(grp_of_row == g, _dot_nt(q8, kc, True), s)
    p = _masked_softmax_rows(s, slope, dist, dist >= 0.0)
    oc = jnp.zeros((NSA_HEADS, LANES), F32)
    for g in range(KV_GROUPS):
        oc = jnp.where(grp_of_row == g, _dot(p, kcs[g], True), oc)
    oc_ref[0] = oc

    p_grp = jnp.concatenate([jnp.sum(p[g * REP:(g + 1) * REP], axis=0, keepdims=True) for g in range(KV_GROUPS)]
                            + [jnp.zeros((NSA_HEADS - KV_GROUPS, n_chunks), F32)], axis=0)
    ov = ov_ref[...]
    p_slc = sum(jnp.dot(part, ov, preferred_element_type=F32) for part in _split3(p_grp))
    jl = lax.broadcasted_iota(jnp.int32, (1, nsp), 1)
    jp = lax.broadcasted_iota(jnp.int32, (nsp, 1), 0)
    cur = past // SLC_BLOCK
    forced = (jl == 0) | (jl == cur) | (jl == cur - 1)
    valid_j = (jl * SLC_BLOCK <= past) & (jl < n_slc)
    score = jnp.where(forced, jnp.inf, jnp.where(valid_j, p_slc, -jnp.inf))
    k_col = lax.broadcasted_iota(jnp.int32, (topn, 1), 0).astype(F32)
    for g in range(KV_GROUPS):
        row_s = score[g:g + 1]
        col_s = _row_to_col(row_s)
        beats = (col_s > row_s) | ((col_s == row_s) & (jp < jl))
        sel = (jnp.sum(beats.astype(F32), axis=0, keepdims=True) < topn).astype(F32)
        before = jnp.sum(jnp.where(jp < jl, _row_to_col(sel), 0.0), axis=0, keepdims=True)
        onehot = jnp.where((before == k_col) & (sel > 0.5), 1.0, 0.0)
        idx = jnp.sum(onehot * jl.astype(F32), axis=1, keepdims=True)
        ids_ref[0, g] = jnp.broadcast_to(idx, (topn, LANES)).astype(jnp.int32)


def _nsa_sample_select(page_table, q8, pool_c, w01, b1r, w2n, w2t):
    n_dec, n_pages = page_table.shape
    page = pool_c.shape[1]
    past = n_pages * page
    n_chunks = past // CMP_STRIDE
    n_slc = -(-(past + 1) // SLC_BLOCK)
    nsp = -(-n_slc // LANES) * LANES
    topn = min(SLC_TOPN, n_slc)
    ov = np.zeros((n_chunks, nsp), np.float32)
    ov[:, :n_slc] = _overlap_t(n_chunks, n_slc).T
    const2 = lambda i, pt: (0, 0)
    grid_spec = pltpu.PrefetchScalarGridSpec(
        num_scalar_prefetch=1,
        grid=(n_dec,),
        in_specs=[pl.BlockSpec((1, NSA_HEADS, LANES), lambda i, pt: (i, 0, 0)),
                  pl.BlockSpec(memory_space=pl.ANY),
                  pl.BlockSpec(w01.shape, lambda i, pt: (0, 0, 0)),
                  pl.BlockSpec((1, LANES), const2),
                  pl.BlockSpec((LANES, LANES), const2),
                  pl.BlockSpec((LANES, LANES), const2),
                  pl.BlockSpec((n_chunks, nsp), const2)],
        out_specs=[pl.BlockSpec((1, NSA_HEADS, LANES), lambda i, pt: (i, 0, 0)),
                   pl.BlockSpec((1, KV_GROUPS, topn, LANES), lambda i, pt: (i, 0, 0, 0))],
        scratch_shapes=[pltpu.VMEM((2, KV_GROUPS, past, LANES), F32),
                        pltpu.SemaphoreType.DMA((2,))],
    )
    return pl.pallas_call(
        _nsa_sample_select_kernel,
        grid_spec=grid_spec,
        out_shape=[jax.ShapeDtypeStruct((n_dec, NSA_HEADS, LANES), F32),
                   jax.ShapeDtypeStruct((n_dec, KV_GROUPS, topn, LANES), jnp.int32)],
        compiler_params=_cparams("arbitrary"),
        name="nsa_sample_select",
    )(page_table, q8, pool_c, w01, b1r, w2n, w2t, jnp.asarray(ov, BF16))


def _block_copies(pool_ref, pt_ref, ids_ref, s_buf, sem, b, slot, n_past_blocks, wait):
    topn = ids_ref.shape[1] // KV_GROUPS
    per_page = pool_ref.shape[1] // SLC_BLOCK
    for g in range(KV_GROUPS):
        for k in range(topn):
            j = ids_ref[b, g * topn + k]

            @pl.when(j < n_past_blocks)
            def _(g=g, k=k, j=j):
                off = pl.multiple_of((j % per_page) * SLC_BLOCK, SLC_BLOCK)
                cp = pltpu.make_async_copy(
                    pool_ref.at[pt_ref[b, j // per_page], pl.ds(off, SLC_BLOCK), pl.ds(g * LANES, LANES)],
                    s_buf.at[slot, g * topn + k], sem.at[slot])
                if wait:
                    cp.wait()
                else:
                    cp.start()


def _nsa_sample_attend_kernel(pt_ref, ids_ref, q_ref, oc_ref, gate_ref, ks_new_ref, kw_new_ref, win_ref, pool_ref,
                              o_ref, nw_ref, s_buf, sem):
    b = pl.program_id(0)
    slot = b % 2
    past = pt_ref.shape[1] * pool_ref.shape[1]
    n_past_blocks = past // SLC_BLOCK
    topn = ids_ref.shape[1] // KV_GROUPS
    n_keys = topn * SLC_BLOCK

    @pl.when(b == 0)
    def _():
        _block_copies(pool_ref, pt_ref, ids_ref, s_buf, sem, b, slot, n_past_blocks, False)

    @pl.when(b + 1 < pl.num_programs(0))
    def _():
        _block_copies(pool_ref, pt_ref, ids_ref, s_buf, sem, b + 1, 1 - slot, n_past_blocks, False)

    _block_copies(pool_ref, pt_ref, ids_ref, s_buf, sem, b, slot, n_past_blocks, True)

    for g in range(KV_GROUPS):
        for k in range(topn):
            @pl.when(ids_ref[b, g * topn + k] >= n_past_blocks)
            def _(g=g, k=k):
                s_buf[slot, g * topn + k] = jnp.zeros((SLC_BLOCK, LANES), F32)
                s_buf[slot, g * topn + k, 0:1, :] = ks_new_ref[0, :, g * LANES:(g + 1) * LANES]

    q8 = q_ref[0]
    grp_of_row, slope = _head_consts()

    lane = lax.broadcasted_iota(jnp.int32, (1, n_keys), 1)
    s = jnp.zeros((NSA_HEADS, n_keys), F32)
    dist = jnp.zeros((NSA_HEADS, n_keys), F32)
    keys = []
    for g in range(KV_GROUPS):
        kg = s_buf[slot, g * topn:(g + 1) * topn].reshape(n_keys, LANES)
        keys.append(kg)
        pos = lane % SLC_BLOCK
        for k in range(topn):
            pos = pos + jnp.where(lane // SLC_BLOCK == k, ids_ref[b, g * topn + k] * SLC_BLOCK, 0)
        s = jnp.where(grp_of_row == g, _dot_nt(q8, kg, True), s)
        dist = jnp.where(grp_of_row == g, (past - pos).astype(F32), dist)
    p = _masked_softmax_rows(s, slope, dist, dist >= 0.0)
    o_slc = jnp.zeros((NSA_HEADS, LANES), F32)
    for g in range(KV_GROUPS):
        o_slc = jnp.where(grp_of_row == g, _dot(p, keys[g], True), o_slc)

    wl = win_ref.shape[1]
    nw_ref[0, 0:wl - 1, :] = win_ref[0, 1:wl, :]
    nw_ref[0, wl - 1:wl, :] = kw_new_ref[0]
    dist_w = (wl - 1 - lax.broadcasted_iota(jnp.int32, (1, wl), 1)).astype(F32)
    s = jnp.zeros((NSA_HEADS, wl), F32)
    for g in range(KV_GROUPS):
        s = jnp.where(grp_of_row == g, _dot_nt(q8, nw_ref[0, :, g * LANES:(g + 1) * LANES], True), s)
    p = _masked_softmax_rows(s, slope, dist_w, dist_w >= 0.0)
    o_win = jnp.zeros((NSA_HEADS, LANES), F32)
    for g in range(KV_GROUPS):
        o_win = jnp.where(grp_of_row == g, _dot(p, nw_ref[0, :, g * LANES:(g + 1) * LANES], True), o_win)

    gates = jax.nn.sigmoid(gate_ref[0])
    o_ref[0] = gates[:, 0:1] * oc_ref[0] + gates[:, 1:2] * o_slc + gates[:, 2:3] * o_win


def _nsa_sample_attend(page_table, ids, q8, oc, gates8, ks_new, kw_new, win_buf, pool_s):
    n_dec = page_table.shape[0]
    wl = win_buf.shape[1]
    topn = ids.shape[1] // KV_GROUPS
    head = pl.BlockSpec((1, NSA_HEADS, LANES), lambda i, pt, sel: (i, 0, 0))
    new_row = pl.BlockSpec((1, 1, KV_COLS), lambda i, pt, sel: (i, 0, 0))
    win = pl.BlockSpec((1, wl, KV_COLS), lambda i, pt, sel: (i, 0, 0))
    grid_spec = pltpu.PrefetchScalarGridSpec(
        num_scalar_prefetch=2,
        grid=(n_dec,),
        in_specs=[head, head, head, new_row, new_row, win, pl.BlockSpec(memory_space=pl.ANY)],
        out_specs=[head, win],
        scratch_shapes=[pltpu.VMEM((2, KV_GROUPS * topn, SLC_BLOCK, LANES), F32),
                        pltpu.SemaphoreType.DMA((2,))],
    )
    return pl.pallas_call(
        _nsa_sample_attend_kernel,
        grid_spec=grid_spec,
        out_shape=[jax.ShapeDtypeStruct((n_dec, NSA_HEADS, LANES), F32),
                   jax.ShapeDtypeStruct((n_dec, wl, KV_COLS), F32)],
        compiler_params=_cparams("arbitrary"),
        name="nsa_sample_attend",
    )(page_table, ids, q8, oc, gates8, ks_new, kw_new, win_buf, pool_s)


def _hgrn_sample_kernel(layer, hg_ref, lb_ref, g_ref, s_ref, o_ref, sn_ref):
    lb_all = _lower_bound(lb_ref, layer)
    hg = hg_ref[0]
    for h in range(HGRN_HEADS):
        part = lambda p, h=h: hg[p * HGRN_HEADS + h:p * HGRN_HEADS + h + 1]
        lb = lb_all[:, h * HGRN_DK:(h + 1) * HGRN_DK]
        fz = part(1)
        f = lb + (1.0 - lb) * jax.nn.sigmoid(fz)
        k = (1.0 - lb) * jax.nn.sigmoid(-fz)
        s_new = _row_to_col(f) * s_ref[0, h] + _row_to_col(k) * part(2)
        sn_ref[0, h] = s_new
        o = jnp.sum(_row_to_col(_silu(part(0))) * s_new, axis=0, keepdims=True)
        o_ref[0, h:h + 1, :] = _rms(o, g_ref[h:h + 1, :]) * _silu(part(3))


def _hgrn_sample(hg, lb_logits, g_out, state, layer):
    n_dec = hg.shape[0]
    st = pl.BlockSpec((1, HGRN_HEADS, HGRN_DK, HGRN_DV), lambda i: (i, 0, 0, 0))
    return pl.pallas_call(
        functools.partial(_hgrn_sample_kernel, layer),
        grid=(n_dec,),
        in_specs=[pl.BlockSpec((1, 4 * HGRN_HEADS, HGRN_DK), lambda i: (i, 0, 0)),
                  pl.BlockSpec(lb_logits.shape, lambda i: (0, 0)),
                  pl.BlockSpec((HGRN_HEADS, HGRN_DV), lambda i: (0, 0)),
                  st],
        out_specs=[pl.BlockSpec((1, HGRN_HEADS, HGRN_DV), lambda i: (i, 0, 0)), st],
        out_shape=[jax.ShapeDtypeStruct((n_dec, HGRN_HEADS, HGRN_DV), F32),
                   jax.ShapeDtypeStruct(state.shape, F32)],
        compiler_params=_cparams("parallel"),
        name="hgrn_sample",
    )(hg, lb_logits, g_out, state)


def _tile(n, pref):
    return pref if n % pref == 0 else n


def kernel(x_prompt, x_sample, c_prompt, c_sample, cache_cmp_kv, cache_slc_kv, cache_win_kv, state_hgrn, page_table, w_ada, b_ada, g_pre_mix, g_post_mix, g_pre_ffn, g_post_ffn, w_in, w_phi1, b_phi1, w_phi2, g_nsa_out, hgrn_lb_logits, g_hgrn_out, w_out, w_route_group, b_route_group, w_route_expert, b_route_expert, w_exp_gate, w_exp_up, w_exp_down):
    l = 0
    bp, t, d = x_prompt.shape
    n_dec = x_sample.shape[0]
    ada = _adaln(jnp.concatenate([c_prompt, c_sample], axis=0), w_ada[l], b_ada[l])
    sh_a, sc_a, gt_a, sh_f, sc_f, gt_f = [ada[:, None, j * d:(j + 1) * d] for j in range(6)]
    w_r, b_r = _prep_router(w_route_group[l], b_route_group[l], w_route_expert[l], b_route_expert[l])
    w01, b1r, w2n, w2t = _prep_compress(w_phi1[l], b_phi1[l], w_phi2[l])

    tile = 256
    w_nat, w_tr = _prep_w_in(w_in[l])
    (q, kv_c, kv_s, kv_w, ks_b, kw_b, hg, vs_t, vw_t, gates_t) = _inproj_prompt(
        x_prompt, sc_a[:bp], sh_a[:bp], g_pre_mix[l][None], w_nat.astype(BF16), w_tr.astype(BF16), tile)
    kc, kc_t = _compress_prompt(kv_c, w01, b1r, w2n, w2t)
    o_nsa = _nsa_prompt(q, kc, kc_t, ks_b, vs_t, kw_b, vw_t, gates_t, tile)
    o_h, s_p = _hgrn_prompt(hg, hgrn_lb_logits, g_hgrn_out[l], l, _tile(t, 512))
    x1, h2, comb = _postmix(o_nsa, o_h, x_prompt, gt_a[:bp], sc_f[:bp], sh_f[:bp], g_nsa_out[l], g_post_mix[l],
                            g_pre_ffn[l], w_out[l].astype(BF16), w_r, b_r, _tile(t, 512), False)
    wg, wu, wd = _prep_experts(w_exp_gate[l], w_exp_up[l], w_exp_down[l], BF16)
    y_p = _moe(h2, comb, x1, gt_f[:bp], g_post_ffn[l], wg, wu, wd, _tile(t, 512), False)

    kv_shape = (1, bp, t, KV_GROUPS, 2, HEAD_DIM)
    w_keep = min(WINDOW, t)
    p_win = kv_w[:, t - w_keep:].reshape(1, bp, w_keep, KV_GROUPS, 2, HEAD_DIM)

    n_pool, page = cache_cmp_kv.shape[1], cache_cmp_kv.shape[2]
    wl = cache_win_kv.shape[2]
    past = page_table.shape[1] * page
    assert x_sample.shape[1] == 1 and wl == WINDOW and past % SLC_BLOCK == 0 and page % SLC_BLOCK == 0
    xs = x_sample.reshape(n_dec, d)
    proj = _inproj_sample(xs, sc_a[bp:, 0], sh_a[bp:, 0], g_pre_mix[l][None], w_in[l])
    sizes = [NSA_WIDTH, KV_COLS, KV_COLS, KV_COLS, 3 * NSA_HEADS, HGRN_WIDTH, HGRN_WIDTH, HGRN_WIDTH, HGRN_WIDTH]
    offs = np.cumsum([0] + sizes)
    seg = lambda j: proj[:, offs[j]:offs[j + 1]]
    pad_lanes = lambda a: jnp.pad(a, ((0, 0), (0, 0), (0, LANES - a.shape[2])))
    q8 = pad_lanes(seg(0).reshape(n_dec, NSA_HEADS, HEAD_DIM) * (HEAD_DIM ** -0.5))
    gates8 = pad_lanes(seg(4).reshape(n_dec, NSA_HEADS, 3))
    hg_s = proj[:, offs[5]:offs[9]].reshape(n_dec, 4 * HGRN_HEADS, HGRN_DK)
    pool_c = cache_cmp_kv[l].reshape(n_pool, page, KV_COLS)
    pool_s = cache_slc_kv[l].reshape(n_pool, page, KV_COLS)
    oc, ids = _nsa_sample_select(page_table, q8, pool_c, w01, b1r, w2n, w2t)
    o8, s_win = _nsa_sample_attend(page_table, ids[..., 0].reshape(n_dec, -1), q8, oc, gates8, seg(2)[:, None],
                                   seg(3)[:, None], cache_win_kv[l].reshape(n_dec, wl, KV_COLS), pool_s)
    o_nsa_s = o8[:, :, HEAD_DIM:].reshape(1, n_dec, NSA_WIDTH)
    oh_s, s_s = _hgrn_sample(hg_s, hgrn_lb_logits, g_hgrn_out[l], state_hgrn[l], l)
    as_row = lambda a: a[bp:].reshape(1, n_dec, d)
    x1s, h2s, comb_s = _postmix(o_nsa_s, oh_s.reshape(1, n_dec, HGRN_WIDTH), xs[None], as_row(gt_a), as_row(sc_f),
                                as_row(sh_f), g_nsa_out[l], g_post_mix[l], g_pre_ffn[l], w_out[l], w_r, b_r,
                                n_dec, True)
    wg32, wu32, wd32 = _prep_experts(w_exp_gate[l], w_exp_up[l], w_exp_down[l], F32)
    y_s = _moe(h2s, comb_s, x1s, as_row(gt_f), g_post_ffn[l], wg32, wu32, wd32, n_dec, True)

    new_shape = (1, n_dec, 1, KV_GROUPS, 2, HEAD_DIM)
    return (y_p, y_s.reshape(n_dec, 1, d), kv_c.reshape(kv_shape), kv_s.reshape(kv_shape), p_win, s_p[None],
            seg(1).reshape(new_shape), seg(2).reshape(new_shape),
            s_win.reshape(1, n_dec, wl, KV_GROUPS, 2, HEAD_DIM), s_s[None])
```

```python
import functools

import numpy as np
import jax
import jax.numpy as jnp
from jax import lax
from jax.experimental import pallas as pl
from jax.experimental.pallas import tpu as pltpu

F32 = jnp.float32
BF16 = jnp.bfloat16
HIGHEST = lax.Precision.HIGHEST

NSA_HEADS = 8
HEAD_DIM = 64
KV_GROUPS = 2
REP = NSA_HEADS // KV_GROUPS
CMP_BLOCK = 32
CMP_STRIDE = 16
SLC_BLOCK = 64
SLC_TOPN = 16
WINDOW = 512
HGRN_HEADS = 4
HGRN_DK = 128
HGRN_DV = 128
HGRN_SUB = 16
N_GROUPS = 4
EXPERTS_PER_GROUP = 4
N_EXPERTS = N_GROUPS * EXPERTS_PER_GROUP
D_EXPERT = 256
NORM_EPS = 1e-6

LANES = 128
KV_COLS = KV_GROUPS * 2 * HEAD_DIM
NSA_WIDTH = NSA_HEADS * HEAD_DIM
HGRN_WIDTH = HGRN_HEADS * HGRN_DV
GATE_ROWS = 16
LOG2E = 1.4426950408889634
MASKED = -1e30
M_INIT = -0.5e30
VMEM_LIMIT = 56 * 1024 * 1024

NT_DIMS = (((1,), (1,)), ((), ()))
TN_DIMS = (((0,), (0,)), ((), ()))


def _cparams(*sem):
    return pltpu.CompilerParams(dimension_semantics=sem, vmem_limit_bytes=VMEM_LIMIT)


def _rms(x, g):
    return x * lax.rsqrt(jnp.mean(x * x, axis=-1, keepdims=True) + NORM_EPS) * g


def _silu(x):
    return x * jax.nn.sigmoid(x)


def _dot(a, b, precise):
    if precise:
        return jnp.dot(a, b, preferred_element_type=F32, precision=HIGHEST)
    return jnp.dot(a.astype(BF16), b.astype(BF16), preferred_element_type=F32)


def _dot_nt(a, b, precise=False):
    if precise:
        return lax.dot_general(a, b, NT_DIMS, preferred_element_type=F32, precision=HIGHEST)
    return lax.dot_general(a.astype(BF16), b.astype(BF16), NT_DIMS, preferred_element_type=F32)


def _adaln_kernel(c_ref, w_ref, b_ref, o_ref):
    o_ref[...] = _dot(_silu(c_ref[...]), w_ref[...], True) + b_ref[...]


def _adaln(c, w, b):
    n, d = c.shape
    cols = w.shape[1]
    return pl.pallas_call(
        _adaln_kernel,
        grid=(cols // d,),
        in_specs=[pl.BlockSpec((n, d), lambda j: (0, 0)),
                  pl.BlockSpec((d, d), lambda j: (0, j)),
                  pl.BlockSpec((1, d), lambda j: (0, j))],
        out_specs=pl.BlockSpec((n, d), lambda j: (0, j)),
        out_shape=jax.ShapeDtypeStruct((n, cols), F32),
        compiler_params=_cparams("arbitrary"),
        name="adaln",
    )(c, w, b.reshape(1, cols))


Q_COLS = NSA_HEADS * LANES
HG_COLS = 4 * HGRN_WIDTH
NAT_COLS = Q_COLS + 3 * KV_COLS + HG_COLS
TR_ROWS = 2 * KV_COLS + KV_GROUPS * GATE_ROWS


def _prep_w_in(w_in):
    d = w_in.shape[0]
    sizes = [NSA_WIDTH, KV_COLS, KV_COLS, KV_COLS, 3 * NSA_HEADS, HGRN_WIDTH, HGRN_WIDTH, HGRN_WIDTH, HGRN_WIDTH]
    offs = np.cumsum([0] + sizes)
    wq = w_in[:, offs[0]:offs[1]].reshape(d, NSA_HEADS, HEAD_DIM) * (HEAD_DIM ** -0.5 * LOG2E)
    wq = jnp.pad(wq, ((0, 0), (0, 0), (0, LANES - HEAD_DIM))).reshape(d, Q_COLS)
    w_nat = jnp.concatenate([wq, w_in[:, offs[1]:offs[4]], w_in[:, offs[5]:offs[9]]], axis=1)
    wg = w_in[:, offs[4]:offs[5]].reshape(d, KV_GROUPS, REP, 3).transpose(0, 1, 3, 2)
    wg = jnp.pad(wg.reshape(d, KV_GROUPS, 3 * REP), ((0, 0), (0, 0), (0, GATE_ROWS - 3 * REP)))
    w_tr = jnp.concatenate([w_in[:, offs[2]:offs[3]], w_in[:, offs[3]:offs[4]],
                            wg.reshape(d, KV_GROUPS * GATE_ROWS)], axis=1).T
    return w_nat, w_tr


def _inproj_prompt_kernel(x_ref, sc_ref, sh_ref, g_ref, wn_ref, wt_ref,
                          q_ref, kvc_ref, kvs_ref, kvw_ref, ksb_ref, kwb_ref, hg_ref,
                          vst_ref, vwt_ref, gt_ref):
    h = _rms(x_ref[0], g_ref[...]) * (1.0 + sc_ref[0]) + sh_ref[0]
    hb = h.astype(BF16)

    def nat(lo, hi):
        return jnp.dot(hb, wn_ref[:, lo:hi], preferred_element_type=F32)

    q_ref[0] = nat(0, Q_COLS).astype(BF16)
    c = Q_COLS
    kvc_ref[0] = nat(c, c + KV_COLS)
    kvs = nat(c + KV_COLS, c + 2 * KV_COLS)
    kvs_ref[0] = kvs
    ksb_ref[0] = kvs.astype(BF16)
    kvw = nat(c + 2 * KV_COLS, c + 3 * KV_COLS)
    kvw_ref[0] = kvw
    kwb_ref[0] = kvw.astype(BF16)
    hg_ref[0] = nat(c + 3 * KV_COLS, c + 3 * KV_COLS + HG_COLS)

    tr = lax.dot_general(wt_ref[...], hb, NT_DIMS, preferred_element_type=F32)
    row = lax.broadcasted_iota(jnp.int32, (KV_COLS, 1), 0)
    is_k = (row % LANES) < HEAD_DIM
    vst_ref[0, 0] = jnp.where(is_k, 1.0, tr[0:KV_COLS]).astype(BF16)
    vwt_ref[0, 0] = jnp.where(is_k, 1.0, tr[KV_COLS:2 * KV_COLS]).astype(BF16)
    gt_ref[0] = jax.nn.sigmoid(tr[2 * KV_COLS:])


def _inproj_prompt(x, sc, sh, g, w_nat, w_tr, tm):
    b, t, d = x.shape
    nt = t // tm
    row = lambda bi, i: (bi, i, 0)
    const = lambda bi, i: (0, 0)
    out_shapes = [
        jax.ShapeDtypeStruct((b, t, Q_COLS), BF16),
        jax.ShapeDtypeStruct((b, t, KV_COLS), F32),
        jax.ShapeDtypeStruct((b, t, KV_COLS), F32),
        jax.ShapeDtypeStruct((b, t, KV_COLS), F32),
        jax.ShapeDtypeStruct((b, t, KV_COLS), BF16),
        jax.ShapeDtypeStruct((b, t, KV_COLS), BF16),
        jax.ShapeDtypeStruct((b, t, HG_COLS), F32),
        jax.ShapeDtypeStruct((b, nt, KV_COLS, tm), BF16),
        jax.ShapeDtypeStruct((b, nt, KV_COLS, tm), BF16),
        jax.ShapeDtypeStruct((b, KV_GROUPS * GATE_ROWS, t), F32),
    ]
    out_specs = [
        pl.BlockSpec((1, tm, Q_COLS), row),
        pl.BlockSpec((1, tm, KV_COLS), row),
        pl.BlockSpec((1, tm, KV_COLS), row),
        pl.BlockSpec((1, tm, KV_COLS), row),
        pl.BlockSpec((1, tm, KV_COLS), row),
        pl.BlockSpec((1, tm, KV_COLS), row),
        pl.BlockSpec((1, tm, HG_COLS), row),
        pl.BlockSpec((1, 1, KV_COLS, tm), lambda bi, i: (bi, i, 0, 0)),
        pl.BlockSpec((1, 1, KV_COLS, tm), lambda bi, i: (bi, i, 0, 0)),
        pl.BlockSpec((1, KV_GROUPS * GATE_ROWS, tm), lambda bi, i: (bi, 0, i)),
    ]
    return pl.pallas_call(
        _inproj_prompt_kernel,
        grid=(b, nt),
        in_specs=[pl.BlockSpec((1, tm, d), row),
                  pl.BlockSpec((1, 1, d), lambda bi, i: (bi, 0, 0)),
                  pl.BlockSpec((1, 1, d), lambda bi, i: (bi, 0, 0)),
                  pl.BlockSpec((1, d), const),
                  pl.BlockSpec((d, NAT_COLS), const),
                  pl.BlockSpec((TR_ROWS, d), const)],
        out_specs=out_specs,
        out_shape=out_shapes,
        compiler_params=_cparams("parallel", "parallel"),
        name="inproj_prompt",
    )(x, sc, sh, g, w_nat, w_tr)


def _prep_compress(w1, b1, w2):
    ratio = CMP_BLOCK // CMP_STRIDE
    w1r = w1.reshape(2, ratio, CMP_STRIDE, HEAD_DIM, HEAD_DIM)
    eye_k = jnp.eye(2, dtype=F32)
    w01 = jnp.einsum('krsde,kl->skdrle', w1r, eye_k).reshape(CMP_STRIDE, LANES, ratio * LANES)
    w2n = jnp.einsum('kef,kl->kelf', w2, eye_k).reshape(LANES, LANES)
    return w01.astype(BF16), b1.reshape(1, LANES), w2n.astype(BF16), w2n.T.astype(BF16)


def _gelu_tanh(x):
    return 0.5 * x * (1.0 + jnp.tanh(0.7978845608028654 * (x + 0.044715 * (x * x * x))))


def _compress_group(load_rows, n_chunks, w01_ref, b1_ref, w2n_ref, w2t_ref):
    y = jnp.zeros((n_chunks, 2 * LANES), F32)
    for s in range(CMP_STRIDE):
        y = y + jnp.dot(load_rows(s).astype(BF16), w01_ref[s], preferred_element_type=F32)
    nxt = pltpu.roll(y[:, LANES:], n_chunks - 1, 0)
    hb = _gelu_tanh(y[:, :LANES] + nxt + b1_ref[...]).astype(BF16)
    return (jnp.dot(hb, w2n_ref[...], preferred_element_type=F32),
            lax.dot_general(w2t_ref[...], hb, NT_DIMS, preferred_element_type=F32))


def _compress_prompt_kernel(*refs):
    x_refs = refs[:KV_GROUPS]
    w01_ref, b1_ref, w2n_ref, w2t_ref, kc_ref, kct_ref = refs[KV_GROUPS:]
    n_chunks = kc_ref.shape[1]
    for g in range(KV_GROUPS):
        load = lambda s, g=g: x_refs[g][0, pl.ds(s, n_chunks, stride=CMP_STRIDE), :]
        kc, kct = _compress_group(load, n_chunks, w01_ref, b1_ref, w2n_ref, w2t_ref)
        kc_ref[0, :, g * LANES:(g + 1) * LANES] = kc.astype(BF16)
        kct_ref[0, g * LANES:(g + 1) * LANES, :] = kct.astype(BF16)


def _compress_prompt(kv_c, w01, b1r, w2n, w2t):
    b, t, _ = kv_c.shape
    n_chunks = t // CMP_STRIDE
    const2 = lambda i: (0, 0)
    return pl.pallas_call(
        _compress_prompt_kernel,
        grid=(b,),
        in_specs=[pl.BlockSpec((1, t, LANES), lambda i, g=g: (i, 0, g)) for g in range(KV_GROUPS)]
        + [pl.BlockSpec(w01.shape, lambda i: (0, 0, 0)),
           pl.BlockSpec((1, LANES), const2),
           pl.BlockSpec((LANES, LANES), const2),
           pl.BlockSpec((LANES, LANES), const2)],
        out_specs=[pl.BlockSpec((1, n_chunks, KV_COLS), lambda i: (i, 0, 0)),
                   pl.BlockSpec((1, KV_COLS, n_chunks), lambda i: (i, 0, 0))],
        out_shape=[jax.ShapeDtypeStruct((b, n_chunks, KV_COLS), BF16),
                   jax.ShapeDtypeStruct((b, KV_COLS, n_chunks), BF16)],
        compiler_params=_cparams("parallel"),
        name="compress_prompt",
    )(*([kv_c] * KV_GROUPS), w01, b1r, w2n, w2t)


def _overlap_t(n_chunks, n_slc):
    start = np.arange(n_chunks)[None, :] * CMP_STRIDE
    j = np.arange(n_slc)[:, None]
    return ((start < (j + 1) * SLC_BLOCK) & (start + CMP_BLOCK > j * SLC_BLOCK)).astype(np.float32)


def _split3(x):
    hi = x.astype(BF16)
    r1 = x - hi.astype(F32)
    mid = r1.astype(BF16)
    lo = (r1 - mid.astype(F32)).astype(BF16)
    return hi, mid, lo


def _rank_select(score, topn):
    n = score.shape[0]
    j = lax.broadcasted_iota(jnp.int32, score.shape, 0)
    cnt = jnp.zeros(score.shape, F32)
    for jp in range(n):
        row = score[jp:jp + 1, :]
        beats = (row > score) | ((row == score) & (j > jp))
        cnt = cnt + beats.astype(F32)
    return (cnt < topn).astype(F32)


def _old_nsa_prompt_kernel(slopes_ref, q_ref, kc_ref, kct_ref, ks_ref, vst_ref, kw_ref, vwt_ref, gt_ref, ovt_ref,
                           o_ref, sel_ref, m_ref, acc_ref, s0_ref, s1_ref, bias_ref, cbias_ref):
    g = pl.program_id(1)
    i = pl.program_id(2)
    tq = q_ref.shape[1]
    kt = vst_ref.shape[3]
    n_chunks = kc_ref.shape[1]
    n_slc = ovt_ref.shape[0]
    blocks_per_tile = kt // SLC_BLOCK
    q0 = i * tq
    qpos = q0 + lax.broadcasted_iota(jnp.int32, (1, tq), 1)
    qpos_f = qpos.astype(F32)
    slopes = [slopes_ref[g * REP + r] for r in range(REP)]
    qh = [q_ref[0, :, r * LANES:(r + 1) * LANES] for r in range(REP)]

    e_pos = (lax.broadcasted_iota(jnp.int32, (n_chunks, 1), 0) * CMP_STRIDE + (CMP_BLOCK - 1)).astype(F32)

    @pl.when(i == 0)
    def _():
        d0 = (lax.broadcasted_iota(jnp.int32, (kt, tq), 0) - lax.broadcasted_iota(jnp.int32, (kt, tq), 1)).astype(F32)
        dc = e_pos - lax.broadcasted_iota(jnp.int32, (1, tq), 1).astype(F32)
        for r in range(REP):
            base = slopes[r] * d0
            bias_ref[0, r] = jnp.where(d0 > 0.0, base, MASKED)
            bias_ref[1, r] = base
            bias_ref[2, r] = jnp.where(d0 <= 0.0, base, MASKED)
            cbias_ref[r] = slopes[r] * dc

    kc = kc_ref[0]
    kct = kct_ref[0]
    mask_c = jnp.where(e_pos <= qpos_f, 0.0, MASKED)
    s_cmp = [lax.dot_general(kc, qh[r], NT_DIMS, preferred_element_type=F32) for r in range(REP)]
    p_grp = jnp.zeros((n_chunks, tq), F32)
    o_cmp = []
    for r in range(REP):
        u = s_cmp[r] + cbias_ref[r] + mask_c
        m = jnp.max(u, axis=0, keepdims=True)
        m = jnp.where(m > M_INIT, m, 0.0)
        e = jnp.exp(u - m)
        p = e * (1.0 / jnp.maximum(jnp.sum(e, axis=0, keepdims=True), 1e-30))
        p_grp = p_grp + p
        o_cmp.append(jnp.dot(kct, p.astype(BF16), preferred_element_type=F32)[HEAD_DIM:])
    ovt = ovt_ref[...]
    p_slc = sum(jnp.dot(ovt, part, preferred_element_type=F32) for part in _split3(p_grp))
    j = lax.broadcasted_iota(jnp.int32, (n_slc, 1), 0)
    cur = qpos // SLC_BLOCK
    forced = (j == 0) | (j == cur) | (j == cur - 1)
    valid_j = (j * SLC_BLOCK) <= qpos
    score = jnp.where(forced, jnp.inf, jnp.where(valid_j, p_slc, -jnp.inf))
    sel_ref[...] = jnp.where(_rank_select(score, min(SLC_TOPN, n_slc)) > 0.5, 0.0, MASKED)

    win_tiles = WINDOW // kt

    def branch(k_ref, vt_ref, first, use_sel):
        m_ref[...] = jnp.full(m_ref.shape, M_INIT, F32)
        acc_ref[...] = jnp.zeros(acc_ref.shape, F32)

        def scores(tile, dst_ref):
            keys = k_ref[0, pl.ds(pl.multiple_of(tile * kt, kt), kt), :]
            for r in range(REP):
                dst_ref[r] = lax.dot_general(keys, qh[r], NT_DIMS, preferred_element_type=F32)

        def step(t, cur_ref, nxt_ref):
            if nxt_ref is not None:
                scores(t + 1, nxt_ref)
            behind = i - t
            variant = jnp.where(behind == 0, 2, jnp.where(behind == win_tiles, 0, 1)) if not use_sel else \
                jnp.where(behind == 0, 2, 1)
            shift = (-(behind * kt)).astype(F32)
            vals = vt_ref[0, t]
            if use_sel:
                sel_bias = jnp.concatenate(
                    [jnp.broadcast_to(sel_ref[pl.ds(t * blocks_per_tile + b, 1), :], (SLC_BLOCK, tq))
                     for b in range(blocks_per_tile)], axis=0)
            for r in range(REP):
                u = cur_ref[r] + bias_ref[variant, r]
                if use_sel:
                    u = u + sel_bias
                c = slopes[r] * shift
                m_old = m_ref[r, 0:1, :]
                m_new = jnp.maximum(m_old, jnp.max(u, axis=0, keepdims=True) + c)
                p = jnp.exp(u - (m_new - c))
                m_ref[r, 0:1, :] = m_new
                acc_ref[r] = (acc_ref[r] * jnp.exp(m_old - m_new)
                              + jnp.dot(vals, p.astype(BF16), preferred_element_type=F32))

        scores(first, s0_ref)

        n_tiles = i - first + 1

        def pair(pi, carry):
            t = first + 2 * pi
            step(t, s0_ref, s1_ref)
            step(t + 1, s1_ref, s0_ref)
            return carry

        lax.fori_loop(0, (n_tiles - 1) // 2, pair, 0)

        @pl.when(n_tiles % 2 == 1)
        def _():
            step(i, s0_ref, None)

        @pl.when(n_tiles % 2 == 0)
        def _():
            step(i - 1, s0_ref, s1_ref)
            step(i, s1_ref, None)

        return [acc_ref[r, HEAD_DIM:, :] / jnp.maximum(acc_ref[r, 0:1, :], 1e-30) for r in range(REP)]

    o_slc = branch(ks_ref, vst_ref, 0, True)
    o_win = branch(kw_ref, vwt_ref, jnp.maximum(i - win_tiles, 0), False)

    gates = gt_ref[0]
    outs = []
    for r in range(REP):
        outs.append(gates[r:r + 1] * o_cmp[r] + gates[REP + r:REP + r + 1] * o_slc[r]
                    + gates[2 * REP + r:2 * REP + r + 1] * o_win[r])
    o_ref[0] = jnp.concatenate(outs, axis=0).T


RANK_CHUNK = 16


def _rank_select_bias(cnt_ref, score, n_rows, topn):
    n = score.shape[0]
    j = lax.broadcasted_iota(jnp.int32, score.shape, 0)
    cnt_ref[...] = jnp.zeros(score.shape, F32)
    for lo in range(0, n, RANK_CHUNK):
        @pl.when(lo < n_rows)
        def _(lo=lo):
            cnt = cnt_ref[...]
            for jp in range(lo, min(lo + RANK_CHUNK, n)):
                row = score[jp:jp + 1, :]
                beats = (row > score) | ((row == score) & (j > jp))
                cnt = cnt + beats.astype(F32)
            cnt_ref[...] = cnt
    return jnp.where(cnt_ref[...] < topn, 0.0, MASKED)


def _nsa_prompt_kernel(slopes_ref, q_ref, kc_ref, kct_ref, ks_ref, vst_ref, kw_ref, vwt_ref, gt_ref, ovt_ref,
                       o_ref, sel_ref, m_ref, acc_ref, sc_ref, s0_ref, s1_ref, w0_ref, w1_ref, bias_ref, cbias_ref):
    g = pl.program_id(1)
    i = pl.program_id(2)
    tq = q_ref.shape[1]
    kt = vst_ref.shape[3]
    wide = REP * tq
    n_chunks = kc_ref.shape[1]
    n_slc = ovt_ref.shape[0]
    blocks_per_tile = kt // SLC_BLOCK
    win_tiles = WINDOW // kt
    q0 = i * tq
    qpos = q0 + lax.broadcasted_iota(jnp.int32, (1, tq), 1)
    qpos_f = qpos.astype(F32)
    per_head = lambda x: jnp.concatenate([x] * REP, axis=1)
    lane_head = lax.broadcasted_iota(jnp.int32, (1, wide), 1) // tq
    slope_row = jnp.zeros((1, wide), F32)
    for r in range(REP):
        slope_row = jnp.where(lane_head == r, slopes_ref[g * REP + r], slope_row)
    q_all = jnp.concatenate([q_ref[0, :, r * LANES:(r + 1) * LANES] for r in range(REP)], axis=0)
    e_pos = (lax.broadcasted_iota(jnp.int32, (n_chunks, 1), 0) * CMP_STRIDE + (CMP_BLOCK - 1)).astype(F32)

    def scores(k_ref, tile, dst_ref):
        keys = k_ref[0, pl.ds(pl.multiple_of(tile * kt, kt), kt), :]
        dst_ref[...] = lax.dot_general(keys, q_all, NT_DIMS, preferred_element_type=F32)

    first_win = jnp.maximum(i - win_tiles, 0)
    sc_ref[...] = lax.dot_general(kc_ref[0], q_all, NT_DIMS, preferred_element_type=F32)
    scores(ks_ref, 0, s0_ref)
    scores(kw_ref, first_win, w0_ref)

    @pl.when(i == 0)
    def _():
        col = lax.broadcasted_iota(jnp.int32, (1, wide), 1) % tq
        d0 = (lax.broadcasted_iota(jnp.int32, (kt, 1), 0) - col).astype(F32)
        base = slope_row * d0
        bias_ref[0] = jnp.where(d0 > 0.0, base, MASKED)
        bias_ref[1] = base
        bias_ref[2] = jnp.where(d0 <= 0.0, base, MASKED)
        cbias_ref[...] = slope_row * (e_pos - col.astype(F32))

    u = sc_ref[...] + cbias_ref[...] + per_head(jnp.where(e_pos <= qpos_f, 0.0, MASKED))
    m = jnp.max(u, axis=0, keepdims=True)
    m = jnp.where(m > M_INIT, m, 0.0)
    e = jnp.exp2(u - m)
    p = e * (1.0 / jnp.maximum(jnp.sum(e, axis=0, keepdims=True), 1e-30))
    o_cmp = jnp.dot(kct_ref[0], p.astype(BF16), preferred_element_type=F32)[HEAD_DIM:]
    p_grp = sum(p[:, r * tq:(r + 1) * tq] for r in range(REP))
    ovt = ovt_ref[...]
    p_slc = sum(jnp.dot(ovt, part, preferred_element_type=F32) for part in _split3(p_grp))
    j = lax.broadcasted_iota(jnp.int32, (n_slc, 1), 0)
    cur = qpos // SLC_BLOCK
    forced = (j == 0) | (j == cur) | (j == cur - 1)
    valid_j = (j * SLC_BLOCK) <= qpos
    score = jnp.where(forced, jnp.inf, jnp.where(valid_j, p_slc, -jnp.inf))
    n_rows = (q0 + tq + SLC_BLOCK - 1) // SLC_BLOCK
    sel_ref[...] = _rank_select_bias(sel_ref, score, n_rows, min(SLC_TOPN, n_slc))

    def branch(k_ref, vt_ref, first, use_sel, cur0_ref, cur1_ref):
        m_ref[...] = jnp.full(m_ref.shape, M_INIT, F32)
        acc_ref[...] = jnp.zeros(acc_ref.shape, F32)

        def step(t, cur_ref, nxt_ref):
            if nxt_ref is not None:
                scores(k_ref, t + 1, nxt_ref)
            behind = i - t
            if use_sel:
                variant = jnp.where(behind == 0, 2, 1)
            else:
                variant = jnp.where(behind == 0, 2, jnp.where(behind == win_tiles, 0, 1))
            u = cur_ref[...] + bias_ref[variant]
            if use_sel:
                u = u + per_head(jnp.concatenate(
                    [jnp.broadcast_to(sel_ref[pl.ds(t * blocks_per_tile + b, 1), :], (SLC_BLOCK, tq))
                     for b in range(blocks_per_tile)], axis=0))
            c = slope_row * (-(behind * kt)).astype(F32)
            m_old = m_ref[0:1, :]
            m_new = jnp.maximum(m_old, jnp.max(u, axis=0, keepdims=True) + c)
            p = jnp.exp2(u - (m_new - c))
            m_ref[0:1, :] = m_new
            acc_ref[...] = (acc_ref[...] * jnp.exp2(m_old - m_new)
                            + jnp.dot(vt_ref[0, t], p.astype(BF16), preferred_element_type=F32))

        n_tiles = i - first + 1

        def pair(pi, carry):
            t = first + 2 * pi
            step(t, cur0_ref, cur1_ref)
            step(t + 1, cur1_ref, cur0_ref)
            return carry

        lax.fori_loop(0, (n_tiles - 1) // 2, pair, 0)

        @pl.when(n_tiles % 2 == 1)
        def _():
            step(i, cur0_ref, None)

        @pl.when(n_tiles % 2 == 0)
        def _():
            step(i - 1, cur0_ref, cur1_ref)
            step(i, cur1_ref, None)

        return acc_ref[HEAD_DIM:, :] * (1.0 / jnp.maximum(acc_ref[0:1, :], 1e-30))

    o_slc = branch(ks_ref, vst_ref, 0, True, s0_ref, s1_ref)
    o_win = branch(kw_ref, vwt_ref, first_win, False, w0_ref, w1_ref)

    gates = gt_ref[0]
    gate = lambda br: jnp.concatenate([gates[br * REP + r:br * REP + r + 1] for r in range(REP)], axis=1)
    out = gate(0) * o_cmp + gate(1) * o_slc + gate(2) * o_win
    o_ref[0] = jnp.concatenate([out[:, r * tq:(r + 1) * tq] for r in range(REP)], axis=0).T


def _nsa_prompt(q, kc, kct, ksb, vst, kwb, vwt, gates_t, tq):
    b, t, _ = q.shape
    n_chunks = kc.shape[1]
    n_tiles, kt = vst.shape[1], vst.shape[3]
    n_slc = -(-t // SLC_BLOCK)
    wide = REP * tq
    slopes = LOG2E * jnp.exp2(-8.0 * jnp.arange(1, NSA_HEADS + 1, dtype=F32) / NSA_HEADS)
    ovt = jnp.asarray(_overlap_t(n_chunks, n_slc), BF16)
    grp = lambda bi, gi, i, s: (bi, 0, gi)
    grid_spec = pltpu.PrefetchScalarGridSpec(
        num_scalar_prefetch=1,
        grid=(b, KV_GROUPS, t // tq),
        in_specs=[pl.BlockSpec((1, tq, REP * LANES), lambda bi, gi, i, s: (bi, i, gi)),
                  pl.BlockSpec((1, n_chunks, LANES), grp),
                  pl.BlockSpec((1, LANES, n_chunks), lambda bi, gi, i, s: (bi, gi, 0)),
                  pl.BlockSpec((1, t, LANES), grp),
                  pl.BlockSpec((1, n_tiles, LANES, kt), lambda bi, gi, i, s: (bi, 0, gi, 0)),
                  pl.BlockSpec((1, t, LANES), grp),
                  pl.BlockSpec((1, n_tiles, LANES, kt), lambda bi, gi, i, s: (bi, 0, gi, 0)),
                  pl.BlockSpec((1, GATE_ROWS, tq), lambda bi, gi, i, s: (bi, gi, i)),
                  pl.BlockSpec((n_slc, n_chunks), lambda bi, gi, i, s: (0, 0))],
        out_specs=pl.BlockSpec((1, tq, REP * HEAD_DIM), lambda bi, gi, i, s: (bi, i, gi)),
        scratch_shapes=[pltpu.VMEM((n_slc, tq), F32),
                        pltpu.VMEM((8, wide), F32),
                        pltpu.VMEM((LANES, wide), F32),
                        pltpu.VMEM((n_chunks, wide), F32),
                        pltpu.VMEM((kt, wide), F32),
                        pltpu.VMEM((kt, wide), F32),
                        pltpu.VMEM((kt, wide), F32),
                        pltpu.VMEM((kt, wide), F32),
                        pltpu.VMEM((3, kt, wide), F32),
                        pltpu.VMEM((n_chunks, wide), F32)],
    )
    assert kt == tq and WINDOW % kt == 0 and t % tq == 0 and n_slc % 4 == 0
    return pl.pallas_call(
        _nsa_prompt_kernel,
        grid_spec=grid_spec,
        out_shape=jax.ShapeDtypeStruct((b, t, NSA_WIDTH), F32),
        compiler_params=_cparams("parallel", "parallel", "arbitrary"),
        name="nsa_prompt",
    )(slopes, q, kc, kct, ksb, vst, kwb, vwt, gates_t, ovt)


def _lower_bound(lb_ref, layer):
    x = lb_ref[...]
    e = jnp.exp(x - jnp.max(x, axis=0, keepdims=True))
    return jnp.sum(e[0:layer + 1], axis=0, keepdims=True) / jnp.sum(e, axis=0, keepdims=True)


def _row_to_col(row):
    n = row.shape[1]
    eye = lax.broadcasted_iota(jnp.int32, (n, n), 0) == lax.broadcasted_iota(jnp.int32, (n, n), 1)
    return jnp.sum(jnp.where(eye, jnp.broadcast_to(row, (n, n)), 0.0), axis=1, keepdims=True)


def _cumsum_rows(x):
    n = x.shape[0]
    row = lax.broadcasted_iota(jnp.int32, (n, 1), 0)
    shift = 1
    while shift < n:
        x = x + jnp.where(row >= shift, pltpu.roll(x, shift, 0), 0.0)
        shift *= 2
    return x


def _hgrn_prompt_kernel(layer, q_ref, f_ref, i_ref, og_ref, lb_ref, g_ref, o_ref, s_ref, state):
    t_idx = pl.program_id(2)
    blk = HGRN_DK
    n_blk = q_ref.shape[1] // blk
    n_sub = blk // HGRN_SUB

    @pl.when(t_idx == 0)
    def _():
        state[...] = jnp.zeros(state.shape, F32)

    lb = _lower_bound(lb_ref, layer)
    gain = g_ref[0]
    row = lax.broadcasted_iota(jnp.int32, (blk, 1), 0)
    causal = lax.broadcasted_iota(jnp.int32, (blk, blk), 1) <= lax.broadcasted_iota(jnp.int32, (blk, blk), 0)

    def block(bi, carry):
        r0 = pl.multiple_of(bi * blk, blk)
        rows = pl.ds(r0, blk)
        fz = f_ref[0, rows, :]
        q = _silu(q_ref[0, rows, :])
        k = (1.0 - lb) * jax.nn.sigmoid(-fz)
        v = i_ref[0, rows, :].astype(BF16)
        cum = _cumsum_rows(jnp.log(lb + (1.0 - lb) * jax.nn.sigmoid(fz)))
        s0 = state[...]
        a_rows = []
        for sub in range(n_sub):
            lo, hi = sub * HGRN_SUB, (sub + 1) * HGRN_SUB
            ref_row = cum[lo - 1:lo] if sub else jnp.zeros((1, blk), F32)
            qe = q[lo:hi] * jnp.exp(cum[lo:hi] - ref_row)
            ke = k * jnp.exp(jnp.where(row < hi, ref_row - cum, MASKED))
            a_rows.append(_dot_nt(qe, ke))
        a = jnp.where(causal, jnp.concatenate(a_rows, axis=0), 0.0)
        o = (jnp.dot(a.astype(BF16), v, preferred_element_type=F32)
             + _dot(q * jnp.exp(cum), s0, False))
        last = cum[blk - 1:blk]
        kd = (k * jnp.exp(last - cum)).astype(BF16)
        state[...] = _row_to_col(jnp.exp(last)) * s0 + lax.dot_general(kd, v, TN_DIMS, preferred_element_type=F32)
        o_ref[0, rows, :] = (_rms(o, gain) * _silu(og_ref[0, rows, :])).astype(o_ref.dtype)
        return carry

    lax.fori_loop(0, n_blk, block, 0)

    @pl.when(t_idx == pl.num_programs(2) - 1)
    def _():
        s_ref[0, 0] = state[...]


def _hgrn_prompt(hg, lb_logits, g_out, layer, tc):
    b, t, _ = hg.shape
    part = lambda p: pl.BlockSpec((1, tc, HGRN_DK), lambda bi, h, i, p=p: (bi, i, p * HGRN_HEADS + h))
    n_layers = lb_logits.shape[0]
    return pl.pallas_call(
        functools.partial(_hgrn_prompt_kernel, layer),
        grid=(b, HGRN_HEADS, t // tc),
        in_specs=[part(0), part(1), part(2), part(3),
                  pl.BlockSpec((n_layers, HGRN_DK), lambda bi, h, i: (0, h)),
                  pl.BlockSpec((1, 1, HGRN_DV), lambda bi, h, i: (h, 0, 0))],
        out_specs=[pl.BlockSpec((1, tc, HGRN_DV), lambda bi, h, i: (bi, i, h)),
                   pl.BlockSpec((1, 1, HGRN_DK, HGRN_DV), lambda bi, h, i: (bi, h, 0, 0))],
        out_shape=[jax.ShapeDtypeStruct((b, t, HGRN_WIDTH), BF16),
                   jax.ShapeDtypeStruct((b, HGRN_HEADS, HGRN_DK, HGRN_DV), F32)],
        scratch_shapes=[pltpu.VMEM((HGRN_DK, HGRN_DV), F32)],
        compiler_params=_cparams("parallel", "parallel", "arbitrary"),
        name="hgrn_prompt",
    )(hg, hg, hg, hg, lb_logits, g_out.reshape(HGRN_HEADS, 1, HGRN_DV))


ROUTE_LANES = LANES
EXPERT_LANE0 = N_GROUPS


def _prep_router(w_rg, b_rg, w_re, b_re):
    d = w_rg.shape[0]
    pad = ROUTE_LANES - N_GROUPS - N_EXPERTS
    w = jnp.concatenate([w_rg, w_re, jnp.zeros((d, pad), F32)], axis=1)
    b = jnp.concatenate([b_rg, b_re, jnp.zeros((pad,), F32)]).reshape(1, ROUTE_LANES)
    return w, b


def _first_lane_of_max(x, lane):
    m = jnp.max(x, axis=1, keepdims=True)
    return m, jnp.min(jnp.where(x == m, lane, float(ROUTE_LANES)), axis=1, keepdims=True)


def _route(logits):
    lane = lax.broadcasted_iota(jnp.int32, (1, ROUTE_LANES), 1).astype(F32)
    is_grp = lane < N_GROUPS
    is_exp = (lane >= EXPERT_LANE0) & (lane < EXPERT_LANE0 + N_EXPERTS)
    lg = jnp.where(is_grp, logits, -jnp.inf)
    mg, g_star = _first_lane_of_max(lg, lane)
    pg_top = 1.0 / jnp.sum(jnp.exp(lg - mg), axis=1, keepdims=True)
    exp_grp = jnp.floor((lane - EXPERT_LANE0) / EXPERTS_PER_GROUP)
    le = jnp.where(is_exp & (exp_grp == g_star), logits, -jnp.inf)
    v1, i1 = _first_lane_of_max(le, lane)
    v2, i2 = _first_lane_of_max(jnp.where(lane == i1, -jnp.inf, le), lane)
    e2 = jnp.exp(v2 - v1)
    w1 = pg_top / (1.0 + e2)
    return jnp.where(lane == i1, w1, 0.0) + jnp.where(lane == i2, w1 * e2, 0.0)


def _postmix_kernel(precise, on_ref, oh_ref, x_ref, gt_ref, sc_ref, sh_ref, gn_ref, gp_ref, gf_ref,
                    wo_ref, wr_ref, br_ref, x1_ref, h2_ref, comb_ref):
    on = _rms(on_ref[0], gn_ref[...])
    mixed = _dot(on, wo_ref[0:NSA_WIDTH, :], precise) + _dot(oh_ref[0], wo_ref[NSA_WIDTH:, :], precise)
    x1 = x_ref[0] + gt_ref[0] * _rms(mixed, gp_ref[...])
    h2 = _rms(x1, gf_ref[...]) * (1.0 + sc_ref[0]) + sh_ref[0]
    x1_ref[0] = x1
    h2_ref[0] = h2.astype(h2_ref.dtype)
    comb_ref[0] = _route(_dot(h2, wr_ref[...], precise) + br_ref[...])


def _postmix(o_nsa, o_h, x, gt, sc, sh, g_nsa, g_post, g_pre_ffn, w_out, w_r, b_r, tm, precise):
    b, t, d = x.shape
    row = lambda bi, i: (bi, i, 0)
    const = lambda bi, i: (0, 0)
    mod = (pl.BlockSpec((1, 1, d), lambda bi, i: (bi, 0, 0)) if gt.shape[1] == 1
           else pl.BlockSpec((1, tm, d), row))
    vec = lambda n: pl.BlockSpec((1, n), const)
    return pl.pallas_call(
        functools.partial(_postmix_kernel, precise),
        grid=(b, t // tm),
        in_specs=[pl.BlockSpec((1, tm, NSA_WIDTH), row), pl.BlockSpec((1, tm, HGRN_WIDTH), row),
                  pl.BlockSpec((1, tm, d), row), mod, mod, mod,
                  vec(NSA_WIDTH), vec(d), vec(d),
                  pl.BlockSpec(w_out.shape, const), pl.BlockSpec(w_r.shape, const), vec(ROUTE_LANES)],
        out_specs=[pl.BlockSpec((1, tm, d), row), pl.BlockSpec((1, tm, d), row),
                   pl.BlockSpec((1, tm, ROUTE_LANES), row)],
        out_shape=[jax.ShapeDtypeStruct((b, t, d), F32),
                   jax.ShapeDtypeStruct((b, t, d), F32 if precise else BF16),
                   jax.ShapeDtypeStruct((b, t, ROUTE_LANES), F32)],
        compiler_params=_cparams("parallel", "parallel"),
        name="postmix",
    )(o_nsa, o_h, x, gt, sc, sh, g_nsa.reshape(1, -1), g_post.reshape(1, -1), g_pre_ffn.reshape(1, -1),
      w_out, w_r, b_r)


def _prep_experts(w_gate, w_up, w_down, dtype):
    e, d, f = w_gate.shape
    side = lambda w: w.reshape(N_GROUPS, EXPERTS_PER_GROUP, d, f).transpose(0, 2, 1, 3).reshape(
        N_GROUPS, d, EXPERTS_PER_GROUP * f).astype(dtype)
    return side(w_gate), side(w_up), w_down.reshape(N_GROUPS, EXPERTS_PER_GROUP * f, d).astype(dtype)


def _moe_kernel(precise, h_ref, comb_ref, x1_ref, gt_ref, gp_ref, wg_ref, wu_ref, wd_ref, y_ref, acc_ref):
    grp = pl.program_id(2)

    @pl.when(grp == 0)
    def _():
        acc_ref[...] = jnp.zeros(acc_ref.shape, F32)

    h = h_ref[0]
    act = _silu(_dot(h, wg_ref[0], precise)) * _dot(h, wu_ref[0], precise)
    lane = lax.broadcasted_iota(jnp.int32, (1, ROUTE_LANES), 1)
    comb = comb_ref[0]
    parts = []
    for e in range(EXPERTS_PER_GROUP):
        sel = lane == EXPERT_LANE0 + grp * EXPERTS_PER_GROUP + e
        cw = jnp.sum(jnp.where(sel, comb, 0.0), axis=1, keepdims=True)
        parts.append(act[:, e * D_EXPERT:(e + 1) * D_EXPERT] * cw)
    acc_ref[...] += _dot(jnp.concatenate(parts, axis=1), wd_ref[0], precise)

    @pl.when(grp == pl.num_programs(2) - 1)
    def _():
        y_ref[0] = x1_ref[0] + gt_ref[0] * _rms(acc_ref[...], gp_ref[...])


def _moe(h2, comb, x1, gt, g_post, wg, wu, wd, tm, precise):
    b, t, d = x1.shape
    row = lambda bi, i, g: (bi, i, 0)
    mod = (pl.BlockSpec((1, 1, d), lambda bi, i, g: (bi, 0, 0)) if gt.shape[1] == 1
           else pl.BlockSpec((1, tm, d), row))
    wide = EXPERTS_PER_GROUP * D_EXPERT
    return pl.pallas_call(
        functools.partial(_moe_kernel, precise),
        grid=(b, t // tm, N_GROUPS),
        in_specs=[pl.BlockSpec((1, tm, d), row), pl.BlockSpec((1, tm, ROUTE_LANES), row),
                  pl.BlockSpec((1, tm, d), row), mod,
                  pl.BlockSpec((1, d), lambda bi, i, g: (0, 0)),
                  pl.BlockSpec((1, d, wide), lambda bi, i, g: (g, 0, 0)),
                  pl.BlockSpec((1, d, wide), lambda bi, i, g: (g, 0, 0)),
                  pl.BlockSpec((1, wide, d), lambda bi, i, g: (g, 0, 0))],
        out_specs=pl.BlockSpec((1, tm, d), row),
        out_shape=jax.ShapeDtypeStruct((b, t, d), F32),
        scratch_shapes=[pltpu.VMEM((tm, d), F32)],
        compiler_params=_cparams("parallel", "parallel", "arbitrary"),
        name="moe",
    )(h2, comb, x1, gt, g_post.reshape(1, -1), wg, wu, wd)


def _inproj_sample_kernel(x_ref, sc_ref, sh_ref, g_ref, w_ref, o_ref):
    h = _rms(x_ref[...], g_ref[...]) * (1.0 + sc_ref[...]) + sh_ref[...]
    o_ref[...] = _dot(h, w_ref[...], True)


def _inproj_sample(x, sc, sh, g, w):
    n, d = x.shape
    cols = w.shape[1]
    full = lambda shape: pl.BlockSpec(shape, lambda i: (0, 0))
    return pl.pallas_call(
        _inproj_sample_kernel,
        grid=(1,),
        in_specs=[full((n, d)), full((n, d)), full((n, d)), full((1, d)), full((d, cols))],
        out_specs=full((n, cols)),
        out_shape=jax.ShapeDtypeStruct((n, cols), F32),
        compiler_params=_cparams("arbitrary"),
        name="inproj_sample",
    )(x, sc, sh, g, w)


def _head_consts():
    row = lax.broadcasted_iota(jnp.int32, (NSA_HEADS, 1), 0)
    slope = jnp.exp2(-(row + 1).astype(F32) * (8.0 / NSA_HEADS))
    return row // REP, slope


def _masked_softmax_rows(s, slope, dist, valid):
    logits = jnp.where(valid, s - slope * dist, MASKED)
    m = jnp.max(logits, axis=1, keepdims=True)
    m = jnp.where(m > M_INIT, m, 0.0)
    e = jnp.exp(logits - m)
    return e / jnp.maximum(jnp.sum(e, axis=1, keepdims=True), 1e-30)


def _dot_split(a, b, nt=False):
    a_hi, a_mid, a_lo = _split3(a)
    b_hi = b.astype(BF16)
    b_lo = (b - b_hi.astype(F32)).astype(BF16)
    n = a.shape[0]
    dims = NT_DIMS if nt else (((1,), (0,)), ((), ()))
    first = lax.dot_general(jnp.concatenate([a_hi, a_mid, a_lo], axis=0), b_hi, dims, preferred_element_type=F32)
    second = lax.dot_general(jnp.concatenate([a_hi, a_mid], axis=0), b_lo, dims, preferred_element_type=F32)
    return first[0:n] + first[n:2 * n] + first[2 * n:3 * n] + second[0:n] + second[n:2 * n]


def _page_copies(pool_ref, pt_ref, xt_buf, sem, b, slot, wait):
    def body(p, carry):
        cp = pltpu.make_async_copy(pool_ref.at[pt_ref[b, p]], xt_buf.at[slot, p], sem.at[slot])
        if wait:
            cp.wait()
        else:
            cp.start()
        return carry

    lax.fori_loop(0, pt_ref.shape[1], body, 0)


def _nsa_sample_select_kernel(pt_ref, q_ref, pool_ref, w01_ref, b1_ref, w2n_ref, w2t_ref, ov_ref,
                              oc_ref, ids_ref, xt_buf, x_buf, sem):
    b = pl.program_id(0)
    slot = b % 2
    n_pages, page = pt_ref.shape[1], pool_ref.shape[2]
    past = n_pages * page
    n_chunks = past // CMP_STRIDE
    n_slc = -(-(past + 1) // SLC_BLOCK)
    nsp = ov_ref.shape[1]
    topn = ids_ref.shape[2]

    @pl.when(b == 0)
    def _():
        _page_copies(pool_ref, pt_ref, xt_buf, sem, b, slot, False)

    @pl.when(b + 1 < pl.num_programs(0))
    def _():
        _page_copies(pool_ref, pt_ref, xt_buf, sem, b + 1, 1 - slot, False)

    _page_copies(pool_ref, pt_ref, xt_buf, sem, b, slot, True)

    def to_rows(p, carry):
        for g in range(KV_GROUPS):
            x_buf[g, pl.ds(pl.multiple_of(p * page, page), page), :] = xt_buf[slot, p, g * LANES:(g + 1) * LANES, :].T
        return carry

    lax.fori_loop(0, n_pages, to_rows, 0)

    q8 = q_ref[0]
    grp_of_row, slope = _head_consts()
    e_pos = lax.broadcasted_iota(jnp.int32, (1, n_chunks), 1) * CMP_STRIDE + (CMP_BLOCK - 1)
    dist = (past - e_pos).astype(F32)
    kcs = []
    s = jnp.zeros((NSA_HEADS, n_chunks), F32)
    for g in range(KV_GROUPS):
        load = lambda sub, g=g: x_buf[g, pl.ds(sub, n_chunks, stride=CMP_STRIDE), :]
        kc, _ = _compress_group(load, n_chunks, w01_ref, b1_ref, w2n_ref, w2t_ref)
        kcs.append(kc)
        s = jnp.where(grp_of_row == g, _dot_split(q8, kc, nt=True), s)
    p = _masked_softmax_rows(s, slope, dist, dist >= 0.0)
    oc = jnp.zeros((NSA_HEADS, LANES), F32)
    for g in range(KV_GROUPS):
        oc = jnp.where(grp_of_row == g, _dot_split(p, kcs[g]), oc)
    oc_ref[0] = oc

    p_grp = jnp.concatenate([jnp.sum(p[g * REP:(g + 1) * REP], axis=0, keepdims=True) for g in range(KV_GROUPS)]
                            + [jnp.zeros((NSA_HEADS - KV_GROUPS, n_chunks), F32)], axis=0)
    ov = ov_ref[...]
    p_slc = sum(jnp.dot(part, ov, preferred_element_type=F32) for part in _split3(p_grp))
    jl = lax.broadcasted_iota(jnp.int32, (1, nsp), 1)
    jp = lax.broadcasted_iota(jnp.int32, (nsp, 1), 0)
    cur = past // SLC_BLOCK
    forced = (jl == 0) | (jl == cur) | (jl == cur - 1)
    valid_j = (jl * SLC_BLOCK <= past) & (jl < n_slc)
    score = jnp.where(forced, jnp.inf, jnp.where(valid_j, p_slc, -jnp.inf))
    k_col = lax.broadcasted_iota(jnp.int32, (topn, 1), 0).astype(F32)
    for g in range(KV_GROUPS):
        row_s = score[g:g + 1]
        col_s = _row_to_col(row_s)
        beats = (col_s > row_s) | ((col_s == row_s) & (jp < jl))
        sel = (jnp.sum(beats.astype(F32), axis=0, keepdims=True) < topn).astype(F32)
        before = jnp.sum(jnp.where(jp < jl, _row_to_col(sel), 0.0), axis=0, keepdims=True)
        onehot = jnp.where((before == k_col) & (sel > 0.5), 1.0, 0.0)
        idx = jnp.sum(onehot * jl.astype(F32), axis=1, keepdims=True)
        ids_ref[0, g] = jnp.broadcast_to(idx, (topn, LANES)).astype(jnp.int32)


def _nsa_sample_select(page_table, q8, pool_c, w01, b1r, w2n, w2t):
    n_dec, n_pages = page_table.shape
    page = pool_c.shape[2]
    past = n_pages * page
    n_chunks = past // CMP_STRIDE
    n_slc = -(-(past + 1) // SLC_BLOCK)
    nsp = -(-n_slc // LANES) * LANES
    topn = min(SLC_TOPN, n_slc)
    ov = np.zeros((n_chunks, nsp), np.float32)
    ov[:, :n_slc] = _overlap_t(n_chunks, n_slc).T
    const2 = lambda i, pt: (0, 0)
    grid_spec = pltpu.PrefetchScalarGridSpec(
        num_scalar_prefetch=1,
        grid=(n_dec,),
        in_specs=[pl.BlockSpec((1, NSA_HEADS, LANES), lambda i, pt: (i, 0, 0)),
                  pl.BlockSpec(memory_space=pl.ANY),
                  pl.BlockSpec(w01.shape, lambda i, pt: (0, 0, 0)),
                  pl.BlockSpec((1, LANES), const2),
                  pl.BlockSpec((LANES, LANES), const2),
                  pl.BlockSpec((LANES, LANES), const2),
                  pl.BlockSpec((n_chunks, nsp), const2)],
        out_specs=[pl.BlockSpec((1, NSA_HEADS, LANES), lambda i, pt: (i, 0, 0)),
                   pl.BlockSpec((1, KV_GROUPS, topn, LANES), lambda i, pt: (i, 0, 0, 0))],
        scratch_shapes=[pltpu.VMEM((2, n_pages, KV_COLS, page), F32),
                        pltpu.VMEM((KV_GROUPS, past, LANES), F32),
                        pltpu.SemaphoreType.DMA((2,))],
    )
    return pl.pallas_call(
        _nsa_sample_select_kernel,
        grid_spec=grid_spec,
        out_shape=[jax.ShapeDtypeStruct((n_dec, NSA_HEADS, LANES), F32),
                   jax.ShapeDtypeStruct((n_dec, KV_GROUPS, topn, LANES), jnp.int32)],
        compiler_params=_cparams("arbitrary"),
        name="nsa_sample_select",
    )(page_table, q8, pool_c, w01, b1r, w2n, w2t, jnp.asarray(ov, BF16))


def _block_copies(pool_ref, pt_ref, ids_ref, s_buf, sem, b, slot, n_past_blocks, wait):
    topn = ids_ref.shape[1] // KV_GROUPS
    per_page = pool_ref.shape[1] // SLC_BLOCK
    for g in range(KV_GROUPS):
        for k in range(topn):
            j = ids_ref[b, g * topn + k]

            @pl.when(j < n_past_blocks)
            def _(g=g, k=k, j=j):
                off = pl.multiple_of((j % per_page) * SLC_BLOCK, SLC_BLOCK)
                cp = pltpu.make_async_copy(
                    pool_ref.at[pt_ref[b, j // per_page], pl.ds(off, SLC_BLOCK), pl.ds(g * LANES, LANES)],
                    s_buf.at[slot, g * topn + k], sem.at[slot])
                if wait:
                    cp.wait()
                else:
                    cp.start()


def _nsa_sample_attend_kernel(pt_ref, ids_ref, q_ref, oc_ref, gate_ref, ks_new_ref, kw_new_ref, win_ref, pool_ref,
                              o_ref, nw_ref, s_buf, sem):
    b = pl.program_id(0)
    slot = b % 2
    past = pt_ref.shape[1] * pool_ref.shape[1]
    n_past_blocks = past // SLC_BLOCK
    topn = ids_ref.shape[1] // KV_GROUPS
    n_keys = topn * SLC_BLOCK

    @pl.when(b == 0)
    def _():
        _block_copies(pool_ref, pt_ref, ids_ref, s_buf, sem, b, slot, n_past_blocks, False)

    @pl.when(b + 1 < pl.num_programs(0))
    def _():
        _block_copies(pool_ref, pt_ref, ids_ref, s_buf, sem, b + 1, 1 - slot, n_past_blocks, False)

    _block_copies(pool_ref, pt_ref, ids_ref, s_buf, sem, b, slot, n_past_blocks, True)

    for g in range(KV_GROUPS):
        for k in range(topn):
            @pl.when(ids_ref[b, g * topn + k] >= n_past_blocks)
            def _(g=g, k=k):
                s_buf[slot, g * topn + k] = jnp.zeros((SLC_BLOCK, LANES), F32)
                s_buf[slot, g * topn + k, 0:1, :] = ks_new_ref[0, :, g * LANES:(g + 1) * LANES]

    q8 = q_ref[0]
    grp_of_row, slope = _head_consts()

    lane = lax.broadcasted_iota(jnp.int32, (1, n_keys), 1)
    s = jnp.zeros((NSA_HEADS, n_keys), F32)
    dist = jnp.zeros((NSA_HEADS, n_keys), F32)
    keys = []
    for g in range(KV_GROUPS):
        kg = s_buf[slot, g * topn:(g + 1) * topn].reshape(n_keys, LANES)
        keys.append(kg)
        pos = lane % SLC_BLOCK
        for k in range(topn):
            pos = pos + jnp.where(lane // SLC_BLOCK == k, ids_ref[b, g * topn + k] * SLC_BLOCK, 0)
        s = jnp.where(grp_of_row == g, _dot_nt(q8, kg, True), s)
        dist = jnp.where(grp_of_row == g, (past - pos).astype(F32), dist)
    p = _masked_softmax_rows(s, slope, dist, dist >= 0.0)
    o_slc = jnp.zeros((NSA_HEADS, LANES), F32)
    for g in range(KV_GROUPS):
        o_slc = jnp.where(grp_of_row == g, _dot(p, keys[g], True), o_slc)

    wl = win_ref.shape[1]
    nw_ref[0, 0:wl - 1, :] = win_ref[0, 1:wl, :]
    nw_ref[0, wl - 1:wl, :] = kw_new_ref[0]
    dist_w = (wl - 1 - lax.broadcasted_iota(jnp.int32, (1, wl), 1)).astype(F32)
    s = jnp.zeros((NSA_HEADS, wl), F32)
    for g in range(KV_GROUPS):
        s = jnp.where(grp_of_row == g, _dot_nt(q8, nw_ref[0, :, g * LANES:(g + 1) * LANES], True), s)
    p = _masked_softmax_rows(s, slope, dist_w, dist_w >= 0.0)
    o_win = jnp.zeros((NSA_HEADS, LANES), F32)
    for g in range(KV_GROUPS):
        o_win = jnp.where(grp_of_row == g, _dot(p, nw_ref[0, :, g * LANES:(g + 1) * LANES], True), o_win)

    gates = jax.nn.sigmoid(gate_ref[0])
    o_ref[0] = gates[:, 0:1] * oc_ref[0] + gates[:, 1:2] * o_slc + gates[:, 2:3] * o_win


def _nsa_sample_attend(page_table, ids, q8, oc, gates8, ks_new, kw_new, win_buf, pool_s):
    n_dec = page_table.shape[0]
    wl = win_buf.shape[1]
    topn = ids.shape[1] // KV_GROUPS
    head = pl.BlockSpec((1, NSA_HEADS, LANES), lambda i, pt, sel: (i, 0, 0))
    new_row = pl.BlockSpec((1, 1, KV_COLS), lambda i, pt, sel: (i, 0, 0))
    win = pl.BlockSpec((1, wl, KV_COLS), lambda i, pt, sel: (i, 0, 0))
    grid_spec = pltpu.PrefetchScalarGridSpec(
        num_scalar_prefetch=2,
        grid=(n_dec,),
        in_specs=[head, head, head, new_row, new_row, win, pl.BlockSpec(memory_space=pl.ANY)],
        out_specs=[head, win],
        scratch_shapes=[pltpu.VMEM((2, KV_GROUPS * topn, SLC_BLOCK, LANES), F32),
                        pltpu.SemaphoreType.DMA((2,))],
    )
    return pl.pallas_call(
        _nsa_sample_attend_kernel,
        grid_spec=grid_spec,
        out_shape=[jax.ShapeDtypeStruct((n_dec, NSA_HEADS, LANES), F32),
                   jax.ShapeDtypeStruct((n_dec, wl, KV_COLS), F32)],
        compiler_params=_cparams("arbitrary"),
        name="nsa_sample_attend",
    )(page_table, ids, q8, oc, gates8, ks_new, kw_new, win_buf, pool_s)


def _page_tile_copies(pool_ref, pt_ref, ids_ref, s_buf, sem, b, slot, n_past_blocks, wait):
    topn = ids_ref.shape[1] // KV_GROUPS
    per_page = pool_ref.shape[2] // SLC_BLOCK
    for g in range(KV_GROUPS):
        for k in range(topn):
            j = ids_ref[b, g * topn + k]

            @pl.when(j < n_past_blocks)
            def _(g=g, k=k, j=j):
                cp = pltpu.make_async_copy(pool_ref.at[pt_ref[b, j // per_page], pl.ds(g * LANES, LANES), :],
                                           s_buf.at[slot, g * topn + k], sem.at[slot])
                if wait:
                    cp.wait()
                else:
                    cp.start()


def _nsa_sample_attend_t_kernel(pt_ref, ids_ref, q_ref, oc_ref, gate_ref, ks_new_ref, kw_new_ref, win_ref, pool_ref,
                                o_ref, nw_ref, s_buf, sem):
    b = pl.program_id(0)
    slot = b % 2
    page = pool_ref.shape[2]
    per_page = page // SLC_BLOCK
    past = pt_ref.shape[1] * page
    n_past_blocks = past // SLC_BLOCK
    topn = ids_ref.shape[1] // KV_GROUPS
    n_keys = topn * page

    @pl.when(b == 0)
    def _():
        _page_tile_copies(pool_ref, pt_ref, ids_ref, s_buf, sem, b, slot, n_past_blocks, False)

    @pl.when(b + 1 < pl.num_programs(0))
    def _():
        _page_tile_copies(pool_ref, pt_ref, ids_ref, s_buf, sem, b + 1, 1 - slot, n_past_blocks, False)

    _page_tile_copies(pool_ref, pt_ref, ids_ref, s_buf, sem, b, slot, n_past_blocks, True)

    first_col = lax.broadcasted_iota(jnp.int32, (1, page), 1) == 0
    for g in range(KV_GROUPS):
        for k in range(topn):
            @pl.when(ids_ref[b, g * topn + k] >= n_past_blocks)
            def _(g=g, k=k):
                col = _row_to_col(ks_new_ref[0, :, g * LANES:(g + 1) * LANES])
                s_buf[slot, g * topn + k] = jnp.where(first_col, col, 0.0)

    q8 = q_ref[0]
    grp_of_row, slope = _head_consts()

    lane = lax.broadcasted_iota(jnp.int32, (1, n_keys), 1)
    tile_of_lane = lane // page
    within = lane % page
    s = jnp.zeros((NSA_HEADS, n_keys), F32)
    dist = jnp.zeros((NSA_HEADS, n_keys), F32)
    valid = jnp.zeros((NSA_HEADS, n_keys), F32)
    tiles = []
    for g in range(KV_GROUPS):
        kg = jnp.concatenate([s_buf[slot, g * topn + k] for k in range(topn)], axis=1)
        tiles.append(kg)
        pos = within
        half = jnp.zeros((1, n_keys), jnp.int32)
        for k in range(topn):
            j = ids_ref[b, g * topn + k]
            pos = pos + jnp.where(tile_of_lane == k, (j // per_page) * page, 0)
            half = jnp.where(tile_of_lane == k, j % per_page, half)
        dist_g = (past - pos).astype(F32)
        valid_g = jnp.where((within // SLC_BLOCK == half) & (dist_g >= 0.0), 1.0, 0.0)
        s = jnp.where(grp_of_row == g, _dot_split(q8, kg), s)
        dist = jnp.where(grp_of_row == g, dist_g, dist)
        valid = jnp.where(grp_of_row == g, valid_g, valid)
    p = _masked_softmax_rows(s, slope, dist, valid > 0.5)
    o_slc = jnp.zeros((NSA_HEADS, LANES), F32)
    for g in range(KV_GROUPS):
        o_slc = jnp.where(grp_of_row == g, _dot_split(p, tiles[g], nt=True), o_slc)

    wl = win_ref.shape[2]
    last_col = lax.broadcasted_iota(jnp.int32, (1, wl), 1) == wl - 1
    new_col = _row_to_col(kw_new_ref[0])
    nw_ref[0] = jnp.where(last_col, new_col, pltpu.roll(win_ref[0], wl - 1, 1))
    dist_w = (wl - 1 - lax.broadcasted_iota(jnp.int32, (1, wl), 1)).astype(F32)
    s = jnp.zeros((NSA_HEADS, wl), F32)
    for g in range(KV_GROUPS):
        s = jnp.where(grp_of_row == g, _dot_split(q8, nw_ref[0, g * LANES:(g + 1) * LANES, :]), s)
    p = _masked_softmax_rows(s, slope, dist_w, dist_w >= 0.0)
    o_win = jnp.zeros((NSA_HEADS, LANES), F32)
    for g in range(KV_GROUPS):
        o_win = jnp.where(grp_of_row == g, _dot_split(p, nw_ref[0, g * LANES:(g + 1) * LANES, :], nt=True), o_win)

    gates = jax.nn.sigmoid(gate_ref[0])
    o_ref[0] = gates[:, 0:1] * oc_ref[0] + gates[:, 1:2] * o_slc + gates[:, 2:3] * o_win


def _nsa_sample_attend_t(page_table, ids, q8, oc, gates8, ks_new, kw_new, win_t, pool_t):
    n_dec = page_table.shape[0]
    wl = win_t.shape[2]
    page = pool_t.shape[2]
    topn = ids.shape[1] // KV_GROUPS
    head = pl.BlockSpec((1, NSA_HEADS, LANES), lambda i, pt, sel: (i, 0, 0))
    new_row = pl.BlockSpec((1, 1, KV_COLS), lambda i, pt, sel: (i, 0, 0))
    win = pl.BlockSpec((1, KV_COLS, wl), lambda i, pt, sel: (i, 0, 0))
    grid_spec = pltpu.PrefetchScalarGridSpec(
        num_scalar_prefetch=2,
        grid=(n_dec,),
        in_specs=[head, head, head, new_row, new_row, win, pl.BlockSpec(memory_space=pl.ANY)],
        out_specs=[head, win],
        scratch_shapes=[pltpu.VMEM((2, KV_GROUPS * topn, LANES, page), F32),
                        pltpu.SemaphoreType.DMA((2,))],
    )
    return pl.pallas_call(
        _nsa_sample_attend_t_kernel,
        grid_spec=grid_spec,
        out_shape=[jax.ShapeDtypeStruct((n_dec, NSA_HEADS, LANES), F32),
                   jax.ShapeDtypeStruct((n_dec, KV_COLS, wl), F32)],
        compiler_params=_cparams("arbitrary"),
        name="nsa_sample_attend",
    )(page_table, ids, q8, oc, gates8, ks_new, kw_new, win_t, pool_t)


def _hgrn_sample_kernel(layer, hg_ref, lb_ref, g_ref, s_ref, o_ref, sn_ref):
    lb_all = _lower_bound(lb_ref, layer)
    hg = hg_ref[0]
    for h in range(HGRN_HEADS):
        part = lambda p, h=h: hg[p * HGRN_HEADS + h:p * HGRN_HEADS + h + 1]
        lb = lb_all[:, h * HGRN_DK:(h + 1) * HGRN_DK]
        fz = part(1)
        f = lb + (1.0 - lb) * jax.nn.sigmoid(fz)
        k = (1.0 - lb) * jax.nn.sigmoid(-fz)
        s_new = _row_to_col(f) * s_ref[0, h] + _row_to_col(k) * part(2)
        sn_ref[0, h] = s_new
        o = jnp.sum(_row_to_col(_silu(part(0))) * s_new, axis=0, keepdims=True)
        o_ref[0, h:h + 1, :] = _rms(o, g_ref[h:h + 1, :]) * _silu(part(3))


def _hgrn_sample(hg, lb_logits, g_out, state, layer):
    n_dec = hg.shape[0]
    st = pl.BlockSpec((1, HGRN_HEADS, HGRN_DK, HGRN_DV), lambda i: (i, 0, 0, 0))
    return pl.pallas_call(
        functools.partial(_hgrn_sample_kernel, layer),
        grid=(n_dec,),
        in_specs=[pl.BlockSpec((1, 4 * HGRN_HEADS, HGRN_DK), lambda i: (i, 0, 0)),
                  pl.BlockSpec(lb_logits.shape, lambda i: (0, 0)),
                  pl.BlockSpec((HGRN_HEADS, HGRN_DV), lambda i: (0, 0)),
                  st],
        out_specs=[pl.BlockSpec((1, HGRN_HEADS, HGRN_DV), lambda i: (i, 0, 0)), st],
        out_shape=[jax.ShapeDtypeStruct((n_dec, HGRN_HEADS, HGRN_DV), F32),
                   jax.ShapeDtypeStruct(state.shape, F32)],
        compiler_params=_cparams("parallel"),
        name="hgrn_sample",
    )(hg, lb_logits, g_out, state)


def _tile(n, pref):
    return pref if n % pref == 0 else n


def kernel(x_prompt, x_sample, c_prompt, c_sample, cache_cmp_kv, cache_slc_kv, cache_win_kv, state_hgrn, page_table, w_ada, b_ada, g_pre_mix, g_post_mix, g_pre_ffn, g_post_ffn, w_in, w_phi1, b_phi1, w_phi2, g_nsa_out, hgrn_lb_logits, g_hgrn_out, w_out, w_route_group, b_route_group, w_route_expert, b_route_expert, w_exp_gate, w_exp_up, w_exp_down):
    l = 0
    bp, t, d = x_prompt.shape
    n_dec = x_sample.shape[0]
    ada = _adaln(jnp.concatenate([c_prompt, c_sample], axis=0), w_ada[l], b_ada[l])
    sh_a, sc_a, gt_a, sh_f, sc_f, gt_f = [ada[:, None, j * d:(j + 1) * d] for j in range(6)]
    w_r, b_r = _prep_router(w_route_group[l], b_route_group[l], w_route_expert[l], b_route_expert[l])
    w01, b1r, w2n, w2t = _prep_compress(w_phi1[l], b_phi1[l], w_phi2[l])

    tile = 256
    w_nat, w_tr = _prep_w_in(w_in[l])
    (q, kv_c, kv_s, kv_w, ks_b, kw_b, hg, vs_t, vw_t, gates_t) = _inproj_prompt(
        x_prompt, sc_a[:bp], sh_a[:bp], g_pre_mix[l][None], w_nat.astype(BF16), w_tr.astype(BF16), tile)
    kc, kc_t = _compress_prompt(kv_c, w01, b1r, w2n, w2t)
    o_nsa = _nsa_prompt(q, kc, kc_t, ks_b, vs_t, kw_b, vw_t, gates_t, tile)
    o_h, s_p = _hgrn_prompt(hg, hgrn_lb_logits, g_hgrn_out[l], l, _tile(t, 512))
    x1, h2, comb = _postmix(o_nsa, o_h, x_prompt, gt_a[:bp], sc_f[:bp], sh_f[:bp], g_nsa_out[l], g_post_mix[l],
                            g_pre_ffn[l], w_out[l].astype(BF16), w_r, b_r, _tile(t, 512), False)
    wg, wu, wd = _prep_experts(w_exp_gate[l], w_exp_up[l], w_exp_down[l], BF16)
    y_p = _moe(h2, comb, x1, gt_f[:bp], g_post_ffn[l], wg, wu, wd, _tile(t, 512), False)

    kv_shape = (1, bp, t, KV_GROUPS, 2, HEAD_DIM)
    w_keep = min(WINDOW, t)
    p_win = kv_w[:, t - w_keep:].reshape(1, bp, w_keep, KV_GROUPS, 2, HEAD_DIM)

    n_pool, page = cache_cmp_kv.shape[1], cache_cmp_kv.shape[2]
    wl = cache_win_kv.shape[2]
    past = page_table.shape[1] * page
    assert x_sample.shape[1] == 1 and wl == WINDOW and past % SLC_BLOCK == 0 and page % SLC_BLOCK == 0
    xs = x_sample.reshape(n_dec, d)
    proj = _inproj_sample(xs, sc_a[bp:, 0], sh_a[bp:, 0], g_pre_mix[l][None], w_in[l])
    sizes = [NSA_WIDTH, KV_COLS, KV_COLS, KV_COLS, 3 * NSA_HEADS, HGRN_WIDTH, HGRN_WIDTH, HGRN_WIDTH, HGRN_WIDTH]
    offs = np.cumsum([0] + sizes)
    seg = lambda j: proj[:, offs[j]:offs[j + 1]]
    pad_lanes = lambda a: jnp.pad(a, ((0, 0), (0, 0), (0, LANES - a.shape[2])))
    q8 = pad_lanes(seg(0).reshape(n_dec, NSA_HEADS, HEAD_DIM) * (HEAD_DIM ** -0.5))
    gates8 = pad_lanes(seg(4).reshape(n_dec, NSA_HEADS, 3))
    hg_s = proj[:, offs[5]:offs[9]].reshape(n_dec, 4 * HGRN_HEADS, HGRN_DK)
    tiles = lambda c: jnp.swapaxes(c.reshape(c.shape[0], c.shape[1], KV_COLS), 1, 2)
    oc, ids = _nsa_sample_select(page_table, q8, tiles(cache_cmp_kv[l]), w01, b1r, w2n, w2t)
    o8, s_win_t = _nsa_sample_attend_t(page_table, ids[..., 0].reshape(n_dec, -1), q8, oc, gates8, seg(2)[:, None],
                                       seg(3)[:, None], tiles(cache_win_kv[l]), tiles(cache_slc_kv[l]))
    s_win = jnp.swapaxes(s_win_t, 1, 2)
    o_nsa_s = o8[:, :, HEAD_DIM:].reshape(1, n_dec, NSA_WIDTH)
    oh_s, s_s = _hgrn_sample(hg_s, hgrn_lb_logits, g_hgrn_out[l], state_hgrn[l], l)
    as_row = lambda a: a[bp:].reshape(1, n_dec, d)
    x1s, h2s, comb_s = _postmix(o_nsa_s, oh_s.reshape(1, n_dec, HGRN_WIDTH), xs[None], as_row(gt_a), as_row(sc_f),
                                as_row(sh_f), g_nsa_out[l], g_post_mix[l], g_pre_ffn[l], w_out[l], w_r, b_r,
                                n_dec, True)
    wg32, wu32, wd32 = _prep_experts(w_exp_gate[l], w_exp_up[l], w_exp_down[l], F32)
    y_s = _moe(h2s, comb_s, x1s, as_row(gt_f), g_post_ffn[l], wg32, wu32, wd32, n_dec, True)

    new_shape = (1, n_dec, 1, KV_GROUPS, 2, HEAD_DIM)
    return (y_p, y_s.reshape(n_dec, 1, d), kv_c.reshape(kv_shape), kv_s.reshape(kv_shape), p_win, s_p[None],
            seg(1).reshape(new_shape), seg(2).reshape(new_shape),
            s_win.reshape(1, n_dec, wl, KV_GROUPS, 2, HEAD_DIM), s_s[None])
```

```python
import functools

import numpy as np
import jax
import jax.numpy as jnp
from jax import lax
from jax.experimental import pallas as pl
from jax.experimental.pallas import tpu as pltpu

F32 = jnp.float32
BF16 = jnp.bfloat16
HIGHEST = lax.Precision.HIGHEST

NSA_HEADS = 8
HEAD_DIM = 64
KV_GROUPS = 2
REP = NSA_HEADS // KV_GROUPS
CMP_BLOCK = 32
CMP_STRIDE = 16
SLC_BLOCK = 64
SLC_TOPN = 16
WINDOW = 512
HGRN_HEADS = 4
HGRN_DK = 128
HGRN_DV = 128
HGRN_SUB = 16
N_GROUPS = 4
EXPERTS_PER_GROUP = 4
N_EXPERTS = N_GROUPS * EXPERTS_PER_GROUP
D_EXPERT = 256
NORM_EPS = 1e-6

LANES = 128
KV_COLS = KV_GROUPS * 2 * HEAD_DIM
NSA_WIDTH = NSA_HEADS * HEAD_DIM
HGRN_WIDTH = HGRN_HEADS * HGRN_DV
GATE_ROWS = 16
LOG2E = 1.4426950408889634
MASKED = -1e30
M_INIT = -0.5e30
VMEM_LIMIT = 56 * 1024 * 1024

NT_DIMS = (((1,), (1,)), ((), ()))
TN_DIMS = (((0,), (0,)), ((), ()))


def _cparams(*sem):
    return pltpu.CompilerParams(dimension_semantics=sem, vmem_limit_bytes=VMEM_LIMIT)


def _rms(x, g):
    return x * lax.rsqrt(jnp.mean(x * x, axis=-1, keepdims=True) + NORM_EPS) * g


def _silu(x):
    return x * jax.nn.sigmoid(x)


def _dot(a, b, precise):
    if precise:
        return jnp.dot(a, b, preferred_element_type=F32, precision=HIGHEST)
    return jnp.dot(a.astype(BF16), b.astype(BF16), preferred_element_type=F32)


def _dot_nt(a, b, precise=False):
    if precise:
        return lax.dot_general(a, b, NT_DIMS, preferred_element_type=F32, precision=HIGHEST)
    return lax.dot_general(a.astype(BF16), b.astype(BF16), NT_DIMS, preferred_element_type=F32)


def _adaln_kernel(c_ref, w_ref, b_ref, o_ref):
    o_ref[...] = _dot(_silu(c_ref[...]), w_ref[...], True) + b_ref[...]


def _adaln(c, w, b):
    n, d = c.shape
    cols = w.shape[1]
    return pl.pallas_call(
        _adaln_kernel,
        grid=(cols // d,),
        in_specs=[pl.BlockSpec((n, d), lambda j: (0, 0)),
                  pl.BlockSpec((d, d), lambda j: (0, j)),
                  pl.BlockSpec((1, d), lambda j: (0, j))],
        out_specs=pl.BlockSpec((n, d), lambda j: (0, j)),
        out_shape=jax.ShapeDtypeStruct((n, cols), F32),
        compiler_params=_cparams("arbitrary"),
        name="adaln",
    )(c, w, b.reshape(1, cols))


Q_COLS = NSA_HEADS * LANES
HG_COLS = 4 * HGRN_WIDTH
NAT_COLS = Q_COLS + 3 * KV_COLS + HG_COLS
TR_ROWS = 2 * KV_COLS + KV_GROUPS * GATE_ROWS


def _prep_w_in(w_in):
    d = w_in.shape[0]
    sizes = [NSA_WIDTH, KV_COLS, KV_COLS, KV_COLS, 3 * NSA_HEADS, HGRN_WIDTH, HGRN_WIDTH, HGRN_WIDTH, HGRN_WIDTH]
    offs = np.cumsum([0] + sizes)
    wq = w_in[:, offs[0]:offs[1]].reshape(d, NSA_HEADS, HEAD_DIM) * (HEAD_DIM ** -0.5 * LOG2E)
    wq = jnp.pad(wq, ((0, 0), (0, 0), (0, LANES - HEAD_DIM))).reshape(d, Q_COLS)
    w_nat = jnp.concatenate([wq, w_in[:, offs[1]:offs[4]], w_in[:, offs[5]:offs[9]]], axis=1)
    wg = w_in[:, offs[4]:offs[5]].reshape(d, KV_GROUPS, REP, 3).transpose(0, 1, 3, 2)
    wg = jnp.pad(wg.reshape(d, KV_GROUPS, 3 * REP), ((0, 0), (0, 0), (0, GATE_ROWS - 3 * REP)))
    w_tr = jnp.concatenate([w_in[:, offs[2]:offs[3]], w_in[:, offs[3]:offs[4]],
                            wg.reshape(d, KV_GROUPS * GATE_ROWS)], axis=1).T
    return w_nat, w_tr


def _inproj_prompt_kernel(x_ref, sc_ref, sh_ref, g_ref, wn_ref, wt_ref,
                          q_ref, kvc_ref, kvs_ref, kvw_ref, ksb_ref, kwb_ref, hg_ref,
                          vst_ref, vwt_ref, gt_ref):
    h = _rms(x_ref[0], g_ref[...]) * (1.0 + sc_ref[0]) + sh_ref[0]
    hb = h.astype(BF16)

    def nat(lo, hi):
        return jnp.dot(hb, wn_ref[:, lo:hi], preferred_element_type=F32)

    q_ref[0] = nat(0, Q_COLS).astype(BF16)
    c = Q_COLS
    kvc_ref[0] = nat(c, c + KV_COLS)
    kvs = nat(c + KV_COLS, c + 2 * KV_COLS)
    kvs_ref[0] = kvs
    ksb_ref[0] = kvs.astype(BF16)
    kvw = nat(c + 2 * KV_COLS, c + 3 * KV_COLS)
    kvw_ref[0] = kvw
    kwb_ref[0] = kvw.astype(BF16)
    hg_ref[0] = nat(c + 3 * KV_COLS, c + 3 * KV_COLS + HG_COLS)

    tr = lax.dot_general(wt_ref[...], hb, NT_DIMS, preferred_element_type=F32)
    row = lax.broadcasted_iota(jnp.int32, (KV_COLS, 1), 0)
    is_k = (row % LANES) < HEAD_DIM
    vst_ref[0, 0] = jnp.where(is_k, 1.0, tr[0:KV_COLS]).astype(BF16)
    vwt_ref[0, 0] = jnp.where(is_k, 1.0, tr[KV_COLS:2 * KV_COLS]).astype(BF16)
    gt_ref[0] = jax.nn.sigmoid(tr[2 * KV_COLS:])


def _inproj_prompt(x, sc, sh, g, w_nat, w_tr, tm):
    b, t, d = x.shape
    nt = t // tm
    row = lambda bi, i: (bi, i, 0)
    const = lambda bi, i: (0, 0)
    out_shapes = [
        jax.ShapeDtypeStruct((b, t, Q_COLS), BF16),
        jax.ShapeDtypeStruct((b, t, KV_COLS), F32),
        jax.ShapeDtypeStruct((b, t, KV_COLS), F32),
        jax.ShapeDtypeStruct((b, t, KV_COLS), F32),
        jax.ShapeDtypeStruct((b, t, KV_COLS), BF16),
        jax.ShapeDtypeStruct((b, t, KV_COLS), BF16),
        jax.ShapeDtypeStruct((b, t, HG_COLS), F32),
        jax.ShapeDtypeStruct((b, nt, KV_COLS, tm), BF16),
        jax.ShapeDtypeStruct((b, nt, KV_COLS, tm), BF16),
        jax.ShapeDtypeStruct((b, KV_GROUPS * GATE_ROWS, t), F32),
    ]
    out_specs = [
        pl.BlockSpec((1, tm, Q_COLS), row),
        pl.BlockSpec((1, tm, KV_COLS), row),
        pl.BlockSpec((1, tm, KV_COLS), row),
        pl.BlockSpec((1, tm, KV_COLS), row),
        pl.BlockSpec((1, tm, KV_COLS), row),
        pl.BlockSpec((1, tm, KV_COLS), row),
        pl.BlockSpec((1, tm, HG_COLS), row),
        pl.BlockSpec((1, 1, KV_COLS, tm), lambda bi, i: (bi, i, 0, 0)),
        pl.BlockSpec((1, 1, KV_COLS, tm), lambda bi, i: (bi, i, 0, 0)),
        pl.BlockSpec((1, KV_GROUPS * GATE_ROWS, tm), lambda bi, i: (bi, 0, i)),
    ]
    return pl.pallas_call(
        _inproj_prompt_kernel,
        grid=(b, nt),
        in_specs=[pl.BlockSpec((1, tm, d), row),
                  pl.BlockSpec((1, 1, d), lambda bi, i: (bi, 0, 0)),
                  pl.BlockSpec((1, 1, d), lambda bi, i: (bi, 0, 0)),
                  pl.BlockSpec((1, d), const),
                  pl.BlockSpec((d, NAT_COLS), const),
                  pl.BlockSpec((TR_ROWS, d), const)],
        out_specs=out_specs,
        out_shape=out_shapes,
        compiler_params=_cparams("parallel", "parallel"),
        name="inproj_prompt",
    )(x, sc, sh, g, w_nat, w_tr)


def _prep_compress(w1, b1, w2):
    ratio = CMP_BLOCK // CMP_STRIDE
    w1r = w1.reshape(2, ratio, CMP_STRIDE, HEAD_DIM, HEAD_DIM)
    eye_k = jnp.eye(2, dtype=F32)
    w01 = jnp.einsum('krsde,kl->skdrle', w1r, eye_k).reshape(CMP_STRIDE, LANES, ratio * LANES)
    w2n = jnp.einsum('kef,kl->kelf', w2, eye_k).reshape(LANES, LANES)
    return w01.astype(BF16), b1.reshape(1, LANES), w2n.astype(BF16), w2n.T.astype(BF16)


def _gelu_tanh(x):
    return 0.5 * x * (1.0 + jnp.tanh(0.7978845608028654 * (x + 0.044715 * (x * x * x))))


def _compress_group(load_rows, n_chunks, w01_ref, b1_ref, w2n_ref, w2t_ref):
    y = jnp.zeros((n_chunks, 2 * LANES), F32)
    for s in range(CMP_STRIDE):
        y = y + jnp.dot(load_rows(s).astype(BF16), w01_ref[s], preferred_element_type=F32)
    nxt = pltpu.roll(y[:, LANES:], n_chunks - 1, 0)
    hb = _gelu_tanh(y[:, :LANES] + nxt + b1_ref[...]).astype(BF16)
    return (jnp.dot(hb, w2n_ref[...], preferred_element_type=F32),
            lax.dot_general(w2t_ref[...], hb, NT_DIMS, preferred_element_type=F32))


def _compress_prompt_kernel(*refs):
    x_refs = refs[:KV_GROUPS]
    w01_ref, b1_ref, w2n_ref, w2t_ref, kc_ref, kct_ref = refs[KV_GROUPS:]
    n_chunks = kc_ref.shape[1]
    for g in range(KV_GROUPS):
        load = lambda s, g=g: x_refs[g][0, pl.ds(s, n_chunks, stride=CMP_STRIDE), :]
        kc, kct = _compress_group(load, n_chunks, w01_ref, b1_ref, w2n_ref, w2t_ref)
        kc_ref[0, :, g * LANES:(g + 1) * LANES] = kc.astype(BF16)
        kct_ref[0, g * LANES:(g + 1) * LANES, :] = kct.astype(BF16)


def _compress_prompt(kv_c, w01, b1r, w2n, w2t):
    b, t, _ = kv_c.shape
    n_chunks = t // CMP_STRIDE
    const2 = lambda i: (0, 0)
    return pl.pallas_call(
        _compress_prompt_kernel,
        grid=(b,),
        in_specs=[pl.BlockSpec((1, t, LANES), lambda i, g=g: (i, 0, g)) for g in range(KV_GROUPS)]
        + [pl.BlockSpec(w01.shape, lambda i: (0, 0, 0)),
           pl.BlockSpec((1, LANES), const2),
           pl.BlockSpec((LANES, LANES), const2),
           pl.BlockSpec((LANES, LANES), const2)],
        out_specs=[pl.BlockSpec((1, n_chunks, KV_COLS), lambda i: (i, 0, 0)),
                   pl.BlockSpec((1, KV_COLS, n_chunks), lambda i: (i, 0, 0))],
        out_shape=[jax.ShapeDtypeStruct((b, n_chunks, KV_COLS), BF16),
                   jax.ShapeDtypeStruct((b, KV_COLS, n_chunks), BF16)],
        compiler_params=_cparams("parallel"),
        name="compress_prompt",
    )(*([kv_c] * KV_GROUPS), w01, b1r, w2n, w2t)


def _overlap_t(n_chunks, n_slc):
    start = np.arange(n_chunks)[None, :] * CMP_STRIDE
    j = np.arange(n_slc)[:, None]
    return ((start < (j + 1) * SLC_BLOCK) & (start + CMP_BLOCK > j * SLC_BLOCK)).astype(np.float32)


def _split3(x):
    hi = x.astype(BF16)
    r1 = x - hi.astype(F32)
    mid = r1.astype(BF16)
    lo = (r1 - mid.astype(F32)).astype(BF16)
    return hi, mid, lo


RANK_CHUNK = 16


def _rank_select_bias(cnt_ref, score, n_rows, topn):
    n = score.shape[0]
    j = lax.broadcasted_iota(jnp.int32, score.shape, 0)
    cnt_ref[...] = jnp.zeros(score.shape, F32)
    for lo in range(0, n, RANK_CHUNK):
        @pl.when(lo < n_rows)
        def _(lo=lo):
            cnt = cnt_ref[...]
            for jp in range(lo, min(lo + RANK_CHUNK, n)):
                row = score[jp:jp + 1, :]
                beats = (row > score) | ((row == score) & (j > jp))
                cnt = cnt + beats.astype(F32)
            cnt_ref[...] = cnt
    return jnp.where(cnt_ref[...] < topn, 0.0, MASKED)


def _nsa_prompt_kernel(slopes_ref, q_ref, kc_ref, kct_ref, ks_ref, vst_ref, kw_ref, vwt_ref, gt_ref, ovt_ref,
                       o_ref, sel_ref, m_ref, acc_ref, sc_ref, s0_ref, s1_ref, w0_ref, w1_ref, bias_ref, cbias_ref):
    g = pl.program_id(1)
    i = pl.program_id(2)
    tq = q_ref.shape[1]
    kt = vst_ref.shape[3]
    wide = REP * tq
    n_chunks = kc_ref.shape[1]
    n_slc = ovt_ref.shape[0]
    blocks_per_tile = kt // SLC_BLOCK
    win_tiles = WINDOW // kt
    q0 = i * tq
    qpos = q0 + lax.broadcasted_iota(jnp.int32, (1, tq), 1)
    qpos_f = qpos.astype(F32)
    per_head = lambda x: jnp.concatenate([x] * REP, axis=1)
    lane_head = lax.broadcasted_iota(jnp.int32, (1, wide), 1) // tq
    slope_row = jnp.zeros((1, wide), F32)
    for r in range(REP):
        slope_row = jnp.where(lane_head == r, slopes_ref[g * REP + r], slope_row)
    q_all = jnp.concatenate([q_ref[0, :, r * LANES:(r + 1) * LANES] for r in range(REP)], axis=0)
    e_pos = (lax.broadcasted_iota(jnp.int32, (n_chunks, 1), 0) * CMP_STRIDE + (CMP_BLOCK - 1)).astype(F32)

    def scores(k_ref, tile, dst_ref):
        keys = k_ref[0, pl.ds(pl.multiple_of(tile * kt, kt), kt), :]
        dst_ref[...] = lax.dot_general(keys, q_all, NT_DIMS, preferred_element_type=F32)

    first_win = jnp.maximum(i - win_tiles, 0)
    sc_ref[...] = lax.dot_general(kc_ref[0], q_all, NT_DIMS, preferred_element_type=F32)
    scores(ks_ref, 0, s0_ref)
    scores(kw_ref, first_win, w0_ref)

    @pl.when(i == 0)
    def _():
        col = lax.broadcasted_iota(jnp.int32, (1, wide), 1) % tq
        d0 = (lax.broadcasted_iota(jnp.int32, (kt, 1), 0) - col).astype(F32)
        base = slope_row * d0
        bias_ref[0] = jnp.where(d0 > 0.0, base, MASKED)
        bias_ref[1] = base
        bias_ref[2] = jnp.where(d0 <= 0.0, base, MASKED)
        cbias_ref[...] = slope_row * (e_pos - col.astype(F32))

    u = sc_ref[...] + cbias_ref[...] + per_head(jnp.where(e_pos <= qpos_f, 0.0, MASKED))
    m = jnp.max(u, axis=0, keepdims=True)
    m = jnp.where(m > M_INIT, m, 0.0)
    e = jnp.exp2(u - m)
    p = e * (1.0 / jnp.maximum(jnp.sum(e, axis=0, keepdims=True), 1e-30))
    o_cmp = jnp.dot(kct_ref[0], p.astype(BF16), preferred_element_type=F32)[HEAD_DIM:]
    p_grp = sum(p[:, r * tq:(r + 1) * tq] for r in range(REP))
    ovt = ovt_ref[...]
    p_slc = sum(jnp.dot(ovt, part, preferred_element_type=F32) for part in _split3(p_grp))
    j = lax.broadcasted_iota(jnp.int32, (n_slc, 1), 0)
    cur = qpos // SLC_BLOCK
    forced = (j == 0) | (j == cur) | (j == cur - 1)
    valid_j = (j * SLC_BLOCK) <= qpos
    score = jnp.where(forced, jnp.inf, jnp.where(valid_j, p_slc, -jnp.inf))
    n_rows = (q0 + tq + SLC_BLOCK - 1) // SLC_BLOCK
    sel_ref[...] = _rank_select_bias(sel_ref, score, n_rows, min(SLC_TOPN, n_slc))

    def branch(k_ref, vt_ref, first, use_sel, cur0_ref, cur1_ref):
        m_ref[...] = jnp.full(m_ref.shape, M_INIT, F32)
        acc_ref[...] = jnp.zeros(acc_ref.shape, F32)

        def step(t, cur_ref, nxt_ref):
            if nxt_ref is not None:
                scores(k_ref, t + 1, nxt_ref)
            behind = i - t
            if use_sel:
                variant = jnp.where(behind == 0, 2, 1)
            else:
                variant = jnp.where(behind == 0, 2, jnp.where(behind == win_tiles, 0, 1))
            u = cur_ref[...] + bias_ref[variant]
            if use_sel:
                u = u + per_head(jnp.concatenate(
                    [jnp.broadcast_to(sel_ref[pl.ds(t * blocks_per_tile + b, 1), :], (SLC_BLOCK, tq))
                     for b in range(blocks_per_tile)], axis=0))
            c = slope_row * (-(behind * kt)).astype(F32)
            m_old = m_ref[0:1, :]
            m_new = jnp.maximum(m_old, jnp.max(u, axis=0, keepdims=True) + c)
            p = jnp.exp2(u - (m_new - c))
            m_ref[0:1, :] = m_new
            acc_ref[...] = (acc_ref[...] * jnp.exp2(m_old - m_new)
                            + jnp.dot(vt_ref[0, t], p.astype(BF16), preferred_element_type=F32))

        n_tiles = i - first + 1

        def pair(pi, carry):
            t = first + 2 * pi
            step(t, cur0_ref, cur1_ref)
            step(t + 1, cur1_ref, cur0_ref)
            return carry

        lax.fori_loop(0, (n_tiles - 1) // 2, pair, 0)

        @pl.when(n_tiles % 2 == 1)
        def _():
            step(i, cur0_ref, None)

        @pl.when(n_tiles % 2 == 0)
        def _():
            step(i - 1, cur0_ref, cur1_ref)
            step(i, cur1_ref, None)

        return acc_ref[HEAD_DIM:, :] * (1.0 / jnp.maximum(acc_ref[0:1, :], 1e-30))

    o_slc = branch(ks_ref, vst_ref, 0, True, s0_ref, s1_ref)
    o_win = branch(kw_ref, vwt_ref, first_win, False, w0_ref, w1_ref)

    gates = gt_ref[0]
    gate = lambda br: jnp.concatenate([gates[br * REP + r:br * REP + r + 1] for r in range(REP)], axis=1)
    out = gate(0) * o_cmp + gate(1) * o_slc + gate(2) * o_win
    o_ref[0] = jnp.concatenate([out[:, r * tq:(r + 1) * tq] for r in range(REP)], axis=0).T


def _nsa_prompt(q, kc, kct, ksb, vst, kwb, vwt, gates_t, tq):
    b, t, _ = q.shape
    n_chunks = kc.shape[1]
    n_tiles, kt = vst.shape[1], vst.shape[3]
    n_slc = -(-t // SLC_BLOCK)
    wide = REP * tq
    slopes = LOG2E * jnp.exp2(-8.0 * jnp.arange(1, NSA_HEADS + 1, dtype=F32) / NSA_HEADS)
    ovt = jnp.asarray(_overlap_t(n_chunks, n_slc), BF16)
    grp = lambda bi, gi, i, s: (bi, 0, gi)
    grid_spec = pltpu.PrefetchScalarGridSpec(
        num_scalar_prefetch=1,
        grid=(b, KV_GROUPS, t // tq),
        in_specs=[pl.BlockSpec((1, tq, REP * LANES), lambda bi, gi, i, s: (bi, i, gi)),
                  pl.BlockSpec((1, n_chunks, LANES), grp),
                  pl.BlockSpec((1, LANES, n_chunks), lambda bi, gi, i, s: (bi, gi, 0)),
                  pl.BlockSpec((1, t, LANES), grp),
                  pl.BlockSpec((1, n_tiles, LANES, kt), lambda bi, gi, i, s: (bi, 0, gi, 0)),
                  pl.BlockSpec((1, t, LANES), grp),
                  pl.BlockSpec((1, n_tiles, LANES, kt), lambda bi, gi, i, s: (bi, 0, gi, 0)),
                  pl.BlockSpec((1, GATE_ROWS, tq), lambda bi, gi, i, s: (bi, gi, i)),
                  pl.BlockSpec((n_slc, n_chunks), lambda bi, gi, i, s: (0, 0))],
        out_specs=pl.BlockSpec((1, tq, REP * HEAD_DIM), lambda bi, gi, i, s: (bi, i, gi)),
        scratch_shapes=[pltpu.VMEM((n_slc, tq), F32),
                        pltpu.VMEM((8, wide), F32),
                        pltpu.VMEM((LANES, wide), F32),
                        pltpu.VMEM((n_chunks, wide), F32),
                        pltpu.VMEM((kt, wide), F32),
                        pltpu.VMEM((kt, wide), F32),
                        pltpu.VMEM((kt, wide), F32),
                        pltpu.VMEM((kt, wide), F32),
                        pltpu.VMEM((3, kt, wide), F32),
                        pltpu.VMEM((n_chunks, wide), F32)],
    )
    assert kt == tq and WINDOW % kt == 0 and t % tq == 0 and n_slc % 4 == 0
    return pl.pallas_call(
        _nsa_prompt_kernel,
        grid_spec=grid_spec,
        out_shape=jax.ShapeDtypeStruct((b, t, NSA_WIDTH), F32),
        compiler_params=_cparams("parallel", "parallel", "arbitrary"),
        name="nsa_prompt",
    )(slopes, q, kc, kct, ksb, vst, kwb, vwt, gates_t, ovt)


def _lower_bound(lb_ref, layer):
    x = lb_ref[...]
    e = jnp.exp(x - jnp.max(x, axis=0, keepdims=True))
    return jnp.sum(e[0:layer + 1], axis=0, keepdims=True) / jnp.sum(e, axis=0, keepdims=True)


def _row_to_col(row):
    n = row.shape[1]
    eye = lax.broadcasted_iota(jnp.int32, (n, n), 0) == lax.broadcasted_iota(jnp.int32, (n, n), 1)
    return jnp.sum(jnp.where(eye, jnp.broadcast_to(row, (n, n)), 0.0), axis=1, keepdims=True)


def _cumsum_rows(x):
    n = x.shape[0]
    row = lax.broadcasted_iota(jnp.int32, (n, 1), 0)
    shift = 1
    while shift < n:
        x = x + jnp.where(row >= shift, pltpu.roll(x, shift, 0), 0.0)
        shift *= 2
    return x


def _hgrn_prompt_kernel(layer, q_ref, f_ref, i_ref, og_ref, lb_ref, g_ref, o_ref, s_ref, state):
    t_idx = pl.program_id(2)
    blk = HGRN_DK
    n_blk = q_ref.shape[1] // blk
    n_sub = blk // HGRN_SUB

    @pl.when(t_idx == 0)
    def _():
        state[...] = jnp.zeros(state.shape, F32)

    lb = _lower_bound(lb_ref, layer)
    gain = g_ref[0]
    row = lax.broadcasted_iota(jnp.int32, (blk, 1), 0)
    causal = lax.broadcasted_iota(jnp.int32, (blk, blk), 1) <= lax.broadcasted_iota(jnp.int32, (blk, blk), 0)

    s0 = state[...]
    for bi in range(n_blk):
        rows = pl.ds(bi * blk, blk)
        fz = f_ref[0, rows, :]
        q = _silu(q_ref[0, rows, :])
        k = (1.0 - lb) * jax.nn.sigmoid(-fz)
        v = i_ref[0, rows, :].astype(BF16)
        cum = _cumsum_rows(jnp.log(lb + (1.0 - lb) * jax.nn.sigmoid(fz)))
        a_rows = []
        for sub in range(n_sub):
            lo, hi = sub * HGRN_SUB, (sub + 1) * HGRN_SUB
            ref_row = cum[lo - 1:lo] if sub else jnp.zeros((1, blk), F32)
            qe = q[lo:hi] * jnp.exp(cum[lo:hi] - ref_row)
            ke = k * jnp.exp(jnp.where(row < hi, ref_row - cum, MASKED))
            a_rows.append(_dot_nt(qe, ke))
        a = jnp.where(causal, jnp.concatenate(a_rows, axis=0), 0.0)
        o = (jnp.dot(a.astype(BF16), v, preferred_element_type=F32)
             + _dot(q * jnp.exp(cum), s0, False))
        last = cum[blk - 1:blk]
        kd = (k * jnp.exp(last - cum)).astype(BF16)
        s0 = _row_to_col(jnp.exp(last)) * s0 + lax.dot_general(kd, v, TN_DIMS, preferred_element_type=F32)
        o_ref[0, rows, :] = (_rms(o, gain) * _silu(og_ref[0, rows, :])).astype(o_ref.dtype)
    state[...] = s0

    @pl.when(t_idx == pl.num_programs(2) - 1)
    def _():
        s_ref[0, 0] = state[...]


def _hgrn_prompt(hg, lb_logits, g_out, layer, tc):
    b, t, _ = hg.shape
    part = lambda p: pl.BlockSpec((1, tc, HGRN_DK), lambda bi, h, i, p=p: (bi, i, p * HGRN_HEADS + h))
    n_layers = lb_logits.shape[0]
    return pl.pallas_call(
        functools.partial(_hgrn_prompt_kernel, layer),
        grid=(b, HGRN_HEADS, t // tc),
        in_specs=[part(0), part(1), part(2), part(3),
                  pl.BlockSpec((n_layers, HGRN_DK), lambda bi, h, i: (0, h)),
                  pl.BlockSpec((1, 1, HGRN_DV), lambda bi, h, i: (h, 0, 0))],
        out_specs=[pl.BlockSpec((1, tc, HGRN_DV), lambda bi, h, i: (bi, i, h)),
                   pl.BlockSpec((1, 1, HGRN_DK, HGRN_DV), lambda bi, h, i: (bi, h, 0, 0))],
        out_shape=[jax.ShapeDtypeStruct((b, t, HGRN_WIDTH), BF16),
                   jax.ShapeDtypeStruct((b, HGRN_HEADS, HGRN_DK, HGRN_DV), F32)],
        scratch_shapes=[pltpu.VMEM((HGRN_DK, HGRN_DV), F32)],
        compiler_params=_cparams("parallel", "parallel", "arbitrary"),
        name="hgrn_prompt",
    )(hg, hg, hg, hg, lb_logits, g_out.reshape(HGRN_HEADS, 1, HGRN_DV))


ROUTE_LANES = LANES
EXPERT_LANE0 = N_GROUPS


def _prep_router(w_rg, b_rg, w_re, b_re):
    d = w_rg.shape[0]
    pad = ROUTE_LANES - N_GROUPS - N_EXPERTS
    w = jnp.concatenate([w_rg, w_re, jnp.zeros((d, pad), F32)], axis=1)
    b = jnp.concatenate([b_rg, b_re, jnp.zeros((pad,), F32)]).reshape(1, ROUTE_LANES)
    return w, b


def _first_lane_of_max(x, lane):
    m = jnp.max(x, axis=1, keepdims=True)
    return m, jnp.min(jnp.where(x == m, lane, float(ROUTE_LANES)), axis=1, keepdims=True)


def _route(logits):
    lane = lax.broadcasted_iota(jnp.int32, (1, ROUTE_LANES), 1).astype(F32)
    is_grp = lane < N_GROUPS
    is_exp = (lane >= EXPERT_LANE0) & (lane < EXPERT_LANE0 + N_EXPERTS)
    lg = jnp.where(is_grp, logits, -jnp.inf)
    mg, g_star = _first_lane_of_max(lg, lane)
    pg_top = 1.0 / jnp.sum(jnp.exp(lg - mg), axis=1, keepdims=True)
    exp_grp = jnp.floor((lane - EXPERT_LANE0) / EXPERTS_PER_GROUP)
    le = jnp.where(is_exp & (exp_grp == g_star), logits, -jnp.inf)
    v1, i1 = _first_lane_of_max(le, lane)
    v2, i2 = _first_lane_of_max(jnp.where(lane == i1, -jnp.inf, le), lane)
    e2 = jnp.exp(v2 - v1)
    w1 = pg_top / (1.0 + e2)
    return jnp.where(lane == i1, w1, 0.0) + jnp.where(lane == i2, w1 * e2, 0.0)


def _postmix_kernel(precise, on_ref, oh_ref, x_ref, gt_ref, sc_ref, sh_ref, gn_ref, gp_ref, gf_ref,
                    wo_ref, wr_ref, br_ref, x1_ref, h2_ref, comb_ref):
    on = _rms(on_ref[0], gn_ref[...])
    mixed = _dot(on, wo_ref[0:NSA_WIDTH, :], precise) + _dot(oh_ref[0], wo_ref[NSA_WIDTH:, :], precise)
    x1 = x_ref[0] + gt_ref[0] * _rms(mixed, gp_ref[...])
    h2 = _rms(x1, gf_ref[...]) * (1.0 + sc_ref[0]) + sh_ref[0]
    x1_ref[0] = x1
    h2_ref[0] = h2.astype(h2_ref.dtype)
    comb_ref[0] = _route(_dot(h2, wr_ref[...], precise) + br_ref[...])


def _postmix(o_nsa, o_h, x, gt, sc, sh, g_nsa, g_post, g_pre_ffn, w_out, w_r, b_r, tm, precise):
    b, t, d = x.shape
    row = lambda bi, i: (bi, i, 0)
    const = lambda bi, i: (0, 0)
    mod = (pl.BlockSpec((1, 1, d), lambda bi, i: (bi, 0, 0)) if gt.shape[1] == 1
           else pl.BlockSpec((1, tm, d), row))
    vec = lambda n: pl.BlockSpec((1, n), const)
    return pl.pallas_call(
        functools.partial(_postmix_kernel, precise),
        grid=(b, t // tm),
        in_specs=[pl.BlockSpec((1, tm, NSA_WIDTH), row), pl.BlockSpec((1, tm, HGRN_WIDTH), row),
                  pl.BlockSpec((1, tm, d), row), mod, mod, mod,
                  vec(NSA_WIDTH), vec(d), vec(d),
                  pl.BlockSpec(w_out.shape, const), pl.BlockSpec(w_r.shape, const), vec(ROUTE_LANES)],
        out_specs=[pl.BlockSpec((1, tm, d), row), pl.BlockSpec((1, tm, d), row),
                   pl.BlockSpec((1, tm, ROUTE_LANES), row)],
        out_shape=[jax.ShapeDtypeStruct((b, t, d), F32),
                   jax.ShapeDtypeStruct((b, t, d), F32 if precise else BF16),
                   jax.ShapeDtypeStruct((b, t, ROUTE_LANES), F32)],
        compiler_params=_cparams("parallel", "parallel"),
        name="postmix",
    )(o_nsa, o_h, x, gt, sc, sh, g_nsa.reshape(1, -1), g_post.reshape(1, -1), g_pre_ffn.reshape(1, -1),
      w_out, w_r, b_r)


def _prep_experts(w_gate, w_up, w_down, dtype):
    e, d, f = w_gate.shape
    side = lambda w: w.reshape(N_GROUPS, EXPERTS_PER_GROUP, d, f).transpose(0, 2, 1, 3).reshape(
        N_GROUPS, d, EXPERTS_PER_GROUP * f).astype(dtype)
    return side(w_gate), side(w_up), w_down.reshape(N_GROUPS, EXPERTS_PER_GROUP * f, d).astype(dtype)


def _moe_kernel(precise, h_ref, comb_ref, x1_ref, gt_ref, gp_ref, wg_ref, wu_ref, wd_ref, y_ref, acc_ref):
    grp = pl.program_id(2)

    @pl.when(grp == 0)
    def _():
        acc_ref[...] = jnp.zeros(acc_ref.shape, F32)

    h = h_ref[0]
    act = _silu(_dot(h, wg_ref[0], precise)) * _dot(h, wu_ref[0], precise)
    lane = lax.broadcasted_iota(jnp.int32, (1, ROUTE_LANES), 1)
    comb = comb_ref[0]
    parts = []
    for e in range(EXPERTS_PER_GROUP):
        sel = lane == EXPERT_LANE0 + grp * EXPERTS_PER_GROUP + e
        cw = jnp.sum(jnp.where(sel, comb, 0.0), axis=1, keepdims=True)
        parts.append(act[:, e * D_EXPERT:(e + 1) * D_EXPERT] * cw)
    acc_ref[...] += _dot(jnp.concatenate(parts, axis=1), wd_ref[0], precise)

    @pl.when(grp == pl.num_programs(2) - 1)
    def _():
        y_ref[0] = x1_ref[0] + gt_ref[0] * _rms(acc_ref[...], gp_ref[...])


def _moe(h2, comb, x1, gt, g_post, wg, wu, wd, tm, precise):
    b, t, d = x1.shape
    row = lambda bi, i, g: (bi, i, 0)
    mod = (pl.BlockSpec((1, 1, d), lambda bi, i, g: (bi, 0, 0)) if gt.shape[1] == 1
           else pl.BlockSpec((1, tm, d), row))
    wide = EXPERTS_PER_GROUP * D_EXPERT
    return pl.pallas_call(
        functools.partial(_moe_kernel, precise),
        grid=(b, t // tm, N_GROUPS),
        in_specs=[pl.BlockSpec((1, tm, d), row), pl.BlockSpec((1, tm, ROUTE_LANES), row),
                  pl.BlockSpec((1, tm, d), row), mod,
                  pl.BlockSpec((1, d), lambda bi, i, g: (0, 0)),
                  pl.BlockSpec((1, d, wide), lambda bi, i, g: (g, 0, 0)),
                  pl.BlockSpec((1, d, wide), lambda bi, i, g: (g, 0, 0)),
                  pl.BlockSpec((1, wide, d), lambda bi, i, g: (g, 0, 0))],
        out_specs=pl.BlockSpec((1, tm, d), row),
        out_shape=jax.ShapeDtypeStruct((b, t, d), F32),
        scratch_shapes=[pltpu.VMEM((tm, d), F32)],
        compiler_params=_cparams("parallel", "parallel", "arbitrary"),
        name="moe",
    )(h2, comb, x1, gt, g_post.reshape(1, -1), wg, wu, wd)


def _inproj_sample_kernel(x_ref, sc_ref, sh_ref, g_ref, w_ref, o_ref):
    h = _rms(x_ref[...], g_ref[...]) * (1.0 + sc_ref[...]) + sh_ref[...]
    o_ref[...] = _dot(h, w_ref[...], True)


def _inproj_sample(x, sc, sh, g, w):
    n, d = x.shape
    cols = w.shape[1]
    full = lambda shape: pl.BlockSpec(shape, lambda i: (0, 0))
    return pl.pallas_call(
        _inproj_sample_kernel,
        grid=(1,),
        in_specs=[full((n, d)), full((n, d)), full((n, d)), full((1, d)), full((d, cols))],
        out_specs=full((n, cols)),
        out_shape=jax.ShapeDtypeStruct((n, cols), F32),
        compiler_params=_cparams("arbitrary"),
        name="inproj_sample",
    )(x, sc, sh, g, w)


def _head_consts():
    row = lax.broadcasted_iota(jnp.int32, (NSA_HEADS, 1), 0)
    slope = jnp.exp2(-(row + 1).astype(F32) * (8.0 / NSA_HEADS))
    return row // REP, slope


def _masked_softmax_rows(s, slope, dist, valid):
    logits = jnp.where(valid, s - slope * dist, MASKED)
    m = jnp.max(logits, axis=1, keepdims=True)
    m = jnp.where(m > M_INIT, m, 0.0)
    e = jnp.exp(logits - m)
    return e / jnp.maximum(jnp.sum(e, axis=1, keepdims=True), 1e-30)


def _dot_split(a, b, nt=False):
    a_hi, a_mid, a_lo = _split3(a)
    b_hi = b.astype(BF16)
    b_lo = (b - b_hi.astype(F32)).astype(BF16)
    n = a.shape[0]
    dims = NT_DIMS if nt else (((1,), (0,)), ((), ()))
    first = lax.dot_general(jnp.concatenate([a_hi, a_mid, a_lo], axis=0), b_hi, dims, preferred_element_type=F32)
    second = lax.dot_general(jnp.concatenate([a_hi, a_mid], axis=0), b_lo, dims, preferred_element_type=F32)
    return first[0:n] + first[n:2 * n] + first[2 * n:3 * n] + second[0:n] + second[n:2 * n]


def _page_copies(pool_ref, pt_ref, xt_buf, sem, b, slot, wait):
    def body(p, carry):
        cp = pltpu.make_async_copy(pool_ref.at[pt_ref[b, p]], xt_buf.at[slot, p], sem.at[slot])
        if wait:
            cp.wait()
        else:
            cp.start()
        return carry

    lax.fori_loop(0, pt_ref.shape[1], body, 0)


def _nsa_sample_select_kernel(pt_ref, q_ref, pool_ref, w01_ref, b1_ref, w2n_ref, w2t_ref, ov_ref,
                              oc_ref, ids_ref, xt_buf, x_buf, sem):
    b = pl.program_id(0)
    slot = b % 2
    n_pages, page = pt_ref.shape[1], pool_ref.shape[2]
    past = n_pages * page
    n_chunks = past // CMP_STRIDE
    n_slc = -(-(past + 1) // SLC_BLOCK)
    nsp = ov_ref.shape[1]
    topn = ids_ref.shape[2]

    @pl.when(b == 0)
    def _():
        _page_copies(pool_ref, pt_ref, xt_buf, sem, b, slot, False)

    @pl.when(b + 1 < pl.num_programs(0))
    def _():
        _page_copies(pool_ref, pt_ref, xt_buf, sem, b + 1, 1 - slot, False)

    _page_copies(pool_ref, pt_ref, xt_buf, sem, b, slot, True)

    pages_per_iter = 4 if n_pages % 4 == 0 else 1

    def to_rows(pi, carry):
        for d in range(pages_per_iter):
            p = pi * pages_per_iter + d
            for g in range(KV_GROUPS):
                x_buf[g, pl.ds(pl.multiple_of(p * page, page), page), :] = \
                    xt_buf[slot, p, g * LANES:(g + 1) * LANES, :].T
        return carry

    lax.fori_loop(0, n_pages // pages_per_iter, to_rows, 0)

    q8 = q_ref[0]
    grp_of_row, slope = _head_consts()
    e_pos = lax.broadcasted_iota(jnp.int32, (1, n_chunks), 1) * CMP_STRIDE + (CMP_BLOCK - 1)
    dist = (past - e_pos).astype(F32)
    kcs = []
    s = jnp.zeros((NSA_HEADS, n_chunks), F32)
    for g in range(KV_GROUPS):
        load = lambda sub, g=g: x_buf[g, pl.ds(sub, n_chunks, stride=CMP_STRIDE), :]
        kc, _ = _compress_group(load, n_chunks, w01_ref, b1_ref, w2n_ref, w2t_ref)
        kcs.append(kc)
        s = jnp.where(grp_of_row == g, _dot_split(q8, kc, nt=True), s)
    p = _masked_softmax_rows(s, slope, dist, dist >= 0.0)
    oc = jnp.zeros((NSA_HEADS, LANES), F32)
    for g in range(KV_GROUPS):
        oc = jnp.where(grp_of_row == g, _dot_split(p, kcs[g]), oc)
    oc_ref[0] = oc

    p_grp = jnp.concatenate([jnp.sum(p[g * REP:(g + 1) * REP], axis=0, keepdims=True) for g in range(KV_GROUPS)]
                            + [jnp.zeros((NSA_HEADS - KV_GROUPS, n_chunks), F32)], axis=0)
    ov = ov_ref[...]
    p_slc = sum(jnp.dot(part, ov, preferred_element_type=F32) for part in _split3(p_grp))
    jl = lax.broadcasted_iota(jnp.int32, (1, nsp), 1)
    jp = lax.broadcasted_iota(jnp.int32, (nsp, 1), 0)
    cur = past // SLC_BLOCK
    forced = (jl == 0) | (jl == cur) | (jl == cur - 1)
    valid_j = (jl * SLC_BLOCK <= past) & (jl < n_slc)
    score = jnp.where(forced, jnp.inf, jnp.where(valid_j, p_slc, -jnp.inf))
    k_col = lax.broadcasted_iota(jnp.int32, (topn, 1), 0).astype(F32)
    for g in range(KV_GROUPS):
        row_s = score[g:g + 1]
        col_s = _row_to_col(row_s)
        beats = (col_s > row_s) | ((col_s == row_s) & (jp < jl))
        sel = (jnp.sum(beats.astype(F32), axis=0, keepdims=True) < topn).astype(F32)
        before = jnp.sum(jnp.where(jp < jl, _row_to_col(sel), 0.0), axis=0, keepdims=True)
        onehot = jnp.where((before == k_col) & (sel > 0.5), 1.0, 0.0)
        idx = jnp.sum(onehot * jl.astype(F32), axis=1, keepdims=True)
        ids_ref[0, g] = jnp.broadcast_to(idx, (topn, LANES)).astype(jnp.int32)


def _nsa_sample_select(page_table, q8, pool_c, w01, b1r, w2n, w2t):
    n_dec, n_pages = page_table.shape
    page = pool_c.shape[2]
    past = n_pages * page
    n_chunks = past // CMP_STRIDE
    n_slc = -(-(past + 1) // SLC_BLOCK)
    nsp = -(-n_slc // LANES) * LANES
    topn = min(SLC_TOPN, n_slc)
    ov = np.zeros((n_chunks, nsp), np.float32)
    ov[:, :n_slc] = _overlap_t(n_chunks, n_slc).T
    const2 = lambda i, pt: (0, 0)
    grid_spec = pltpu.PrefetchScalarGridSpec(
        num_scalar_prefetch=1,
        grid=(n_dec,),
        in_specs=[pl.BlockSpec((1, NSA_HEADS, LANES), lambda i, pt: (i, 0, 0)),
                  pl.BlockSpec(memory_space=pl.ANY),
                  pl.BlockSpec(w01.shape, lambda i, pt: (0, 0, 0)),
                  pl.BlockSpec((1, LANES), const2),
                  pl.BlockSpec((LANES, LANES), const2),
                  pl.BlockSpec((LANES, LANES), const2),
                  pl.BlockSpec((n_chunks, nsp), const2)],
        out_specs=[pl.BlockSpec((1, NSA_HEADS, LANES), lambda i, pt: (i, 0, 0)),
                   pl.BlockSpec((1, KV_GROUPS, topn, LANES), lambda i, pt: (i, 0, 0, 0))],
        scratch_shapes=[pltpu.VMEM((2, n_pages, KV_COLS, page), F32),
                        pltpu.VMEM((KV_GROUPS, past, LANES), F32),
                        pltpu.SemaphoreType.DMA((2,))],
    )
    return pl.pallas_call(
        _nsa_sample_select_kernel,
        grid_spec=grid_spec,
        out_shape=[jax.ShapeDtypeStruct((n_dec, NSA_HEADS, LANES), F32),
                   jax.ShapeDtypeStruct((n_dec, KV_GROUPS, topn, LANES), jnp.int32)],
        compiler_params=_cparams("arbitrary"),
        name="nsa_sample_select",
    )(page_table, q8, pool_c, w01, b1r, w2n, w2t, jnp.asarray(ov, BF16))


def _page_tile_copies(pool_ref, pt_ref, ids_ref, s_buf, sem, b, slot, n_past_blocks, wait):
    topn = ids_ref.shape[1] // KV_GROUPS
    per_page = pool_ref.shape[2] // SLC_BLOCK
    for g in range(KV_GROUPS):
        for k in range(topn):
            j = ids_ref[b, g * topn + k]

            @pl.when(j < n_past_blocks)
            def _(g=g, k=k, j=j):
                cp = pltpu.make_async_copy(pool_ref.at[pt_ref[b, j // per_page], pl.ds(g * LANES, LANES), :],
                                           s_buf.at[slot, g * topn + k], sem.at[slot])
                if wait:
                    cp.wait()
                else:
                    cp.start()


def _nsa_sample_attend_t_kernel(pt_ref, ids_ref, q_ref, oc_ref, gate_ref, ks_new_ref, kw_new_ref, win_ref, pool_ref,
                                o_ref, nw_ref, s_buf, sem):
    b = pl.program_id(0)
    slot = b % 2
    page = pool_ref.shape[2]
    per_page = page // SLC_BLOCK
    past = pt_ref.shape[1] * page
    n_past_blocks = past // SLC_BLOCK
    topn = ids_ref.shape[1] // KV_GROUPS
    n_keys = topn * page

    @pl.when(b == 0)
    def _():
        _page_tile_copies(pool_ref, pt_ref, ids_ref, s_buf, sem, b, slot, n_past_blocks, False)

    @pl.when(b + 1 < pl.num_programs(0))
    def _():
        _page_tile_copies(pool_ref, pt_ref, ids_ref, s_buf, sem, b + 1, 1 - slot, n_past_blocks, False)

    _page_tile_copies(pool_ref, pt_ref, ids_ref, s_buf, sem, b, slot, n_past_blocks, True)

    first_col = lax.broadcasted_iota(jnp.int32, (1, page), 1) == 0
    for g in range(KV_GROUPS):
        for k in range(topn):
            @pl.when(ids_ref[b, g * topn + k] >= n_past_blocks)
            def _(g=g, k=k):
                col = _row_to_col(ks_new_ref[0, :, g * LANES:(g + 1) * LANES])
                s_buf[slot, g * topn + k] = jnp.where(first_col, col, 0.0)

    q8 = q_ref[0]
    grp_of_row, slope = _head_consts()

    lane = lax.broadcasted_iota(jnp.int32, (1, n_keys), 1)
    tile_of_lane = lane // page
    within = lane % page
    s = jnp.zeros((NSA_HEADS, n_keys), F32)
    dist = jnp.zeros((NSA_HEADS, n_keys), F32)
    valid = jnp.zeros((NSA_HEADS, n_keys), F32)
    tiles = []
    for g in range(KV_GROUPS):
        kg = jnp.concatenate([s_buf[slot, g * topn + k] for k in range(topn)], axis=1)
        tiles.append(kg)
        pos = within
        half = jnp.zeros((1, n_keys), jnp.int32)
        for k in range(topn):
            j = ids_ref[b, g * topn + k]
            pos = pos + jnp.where(tile_of_lane == k, (j // per_page) * page, 0)
            half = jnp.where(tile_of_lane == k, j % per_page, half)
        dist_g = (past - pos).astype(F32)
        valid_g = jnp.where((within // SLC_BLOCK == half) & (dist_g >= 0.0), 1.0, 0.0)
        s = jnp.where(grp_of_row == g, _dot_split(q8, kg), s)
        dist = jnp.where(grp_of_row == g, dist_g, dist)
        valid = jnp.where(grp_of_row == g, valid_g, valid)
    p = _masked_softmax_rows(s, slope, dist, valid > 0.5)
    o_slc = jnp.zeros((NSA_HEADS, LANES), F32)
    for g in range(KV_GROUPS):
        o_slc = jnp.where(grp_of_row == g, _dot_split(p, tiles[g], nt=True), o_slc)

    wl = win_ref.shape[2]
    last_col = lax.broadcasted_iota(jnp.int32, (1, wl), 1) == wl - 1
    new_col = _row_to_col(kw_new_ref[0])
    nw_ref[0] = jnp.where(last_col, new_col, pltpu.roll(win_ref[0], wl - 1, 1))
    dist_w = (wl - 1 - lax.broadcasted_iota(jnp.int32, (1, wl), 1)).astype(F32)
    s = jnp.zeros((NSA_HEADS, wl), F32)
    for g in range(KV_GROUPS):
        s = jnp.where(grp_of_row == g, _dot_split(q8, nw_ref[0, g * LANES:(g + 1) * LANES, :]), s)
    p = _masked_softmax_rows(s, slope, dist_w, dist_w >= 0.0)
    o_win = jnp.zeros((NSA_HEADS, LANES), F32)
    for g in range(KV_GROUPS):
        o_win = jnp.where(grp_of_row == g, _dot_split(p, nw_ref[0, g * LANES:(g + 1) * LANES, :], nt=True), o_win)

    gates = jax.nn.sigmoid(gate_ref[0])
    o_ref[0] = gates[:, 0:1] * oc_ref[0] + gates[:, 1:2] * o_slc + gates[:, 2:3] * o_win


def _nsa_sample_attend_t(page_table, ids, q8, oc, gates8, ks_new, kw_new, win_t, pool_t):
    n_dec = page_table.shape[0]
    wl = win_t.shape[2]
    page = pool_t.shape[2]
    topn = ids.shape[1] // KV_GROUPS
    head = pl.BlockSpec((1, NSA_HEADS, LANES), lambda i, pt, sel: (i, 0, 0))
    new_row = pl.BlockSpec((1, 1, KV_COLS), lambda i, pt, sel: (i, 0, 0))
    win = pl.BlockSpec((1, KV_COLS, wl), lambda i, pt, sel: (i, 0, 0))
    grid_spec = pltpu.PrefetchScalarGridSpec(
        num_scalar_prefetch=2,
        grid=(n_dec,),
        in_specs=[head, head, head, new_row, new_row, win, pl.BlockSpec(memory_space=pl.ANY)],
        out_specs=[head, win],
        scratch_shapes=[pltpu.VMEM((2, KV_GROUPS * topn, LANES, page), F32),
                        pltpu.SemaphoreType.DMA((2,))],
    )
    return pl.pallas_call(
        _nsa_sample_attend_t_kernel,
        grid_spec=grid_spec,
        out_shape=[jax.ShapeDtypeStruct((n_dec, NSA_HEADS, LANES), F32),
                   jax.ShapeDtypeStruct((n_dec, KV_COLS, wl), F32)],
        compiler_params=_cparams("arbitrary"),
        name="nsa_sample_attend",
    )(page_table, ids, q8, oc, gates8, ks_new, kw_new, win_t, pool_t)


def _hgrn_sample_kernel(layer, hg_ref, lb_ref, g_ref, s_ref, o_ref, sn_ref):
    lb_all = _lower_bound(lb_ref, layer)
    hg = hg_ref[0]
    for h in range(HGRN_HEADS):
        part = lambda p, h=h: hg[p * HGRN_HEADS + h:p * HGRN_HEADS + h + 1]
        lb = lb_all[:, h * HGRN_DK:(h + 1) * HGRN_DK]
        fz = part(1)
        f = lb + (1.0 - lb) * jax.nn.sigmoid(fz)
        k = (1.0 - lb) * jax.nn.sigmoid(-fz)
        s_new = _row_to_col(f) * s_ref[0, h] + _row_to_col(k) * part(2)
        sn_ref[0, h] = s_new
        o = jnp.sum(_row_to_col(_silu(part(0))) * s_new, axis=0, keepdims=True)
        o_ref[0, h:h + 1, :] = _rms(o, g_ref[h:h + 1, :]) * _silu(part(3))


def _hgrn_sample(hg, lb_logits, g_out, state, layer):
    n_dec = hg.shape[0]
    st = pl.BlockSpec((1, HGRN_HEADS, HGRN_DK, HGRN_DV), lambda i: (i, 0, 0, 0))
    return pl.pallas_call(
        functools.partial(_hgrn_sample_kernel, layer),
        grid=(n_dec,),
        in_specs=[pl.BlockSpec((1, 4 * HGRN_HEADS, HGRN_DK), lambda i: (i, 0, 0)),
                  pl.BlockSpec(lb_logits.shape, lambda i: (0, 0)),
                  pl.BlockSpec((HGRN_HEADS, HGRN_DV), lambda i: (0, 0)),
                  st],
        out_specs=[pl.BlockSpec((1, HGRN_HEADS, HGRN_DV), lambda i: (i, 0, 0)), st],
        out_shape=[jax.ShapeDtypeStruct((n_dec, HGRN_HEADS, HGRN_DV), F32),
                   jax.ShapeDtypeStruct(state.shape, F32)],
        compiler_params=_cparams("parallel"),
        name="hgrn_sample",
    )(hg, lb_logits, g_out, state)


def _tile(n, pref):
    return pref if n % pref == 0 else n


def kernel(x_prompt, x_sample, c_prompt, c_sample, cache_cmp_kv, cache_slc_kv, cache_win_kv, state_hgrn, page_table, w_ada, b_ada, g_pre_mix, g_post_mix, g_pre_ffn, g_post_ffn, w_in, w_phi1, b_phi1, w_phi2, g_nsa_out, hgrn_lb_logits, g_hgrn_out, w_out, w_route_group, b_route_group, w_route_expert, b_route_expert, w_exp_gate, w_exp_up, w_exp_down):
    l = 0
    bp, t, d = x_prompt.shape
    n_dec = x_sample.shape[0]
    ada = _adaln(jnp.concatenate([c_prompt, c_sample], axis=0), w_ada[l], b_ada[l])
    sh_a, sc_a, gt_a, sh_f, sc_f, gt_f = [ada[:, None, j * d:(j + 1) * d] for j in range(6)]
    w_r, b_r = _prep_router(w_route_group[l], b_route_group[l], w_route_expert[l], b_route_expert[l])
    w01, b1r, w2n, w2t = _prep_compress(w_phi1[l], b_phi1[l], w_phi2[l])

    tile = 256
    w_nat, w_tr = _prep_w_in(w_in[l])
    (q, kv_c, kv_s, kv_w, ks_b, kw_b, hg, vs_t, vw_t, gates_t) = _inproj_prompt(
        x_prompt, sc_a[:bp], sh_a[:bp], g_pre_mix[l][None], w_nat.astype(BF16), w_tr.astype(BF16), tile)
    kc, kc_t = _compress_prompt(kv_c, w01, b1r, w2n, w2t)
    o_nsa = _nsa_prompt(q, kc, kc_t, ks_b, vs_t, kw_b, vw_t, gates_t, tile)
    o_h, s_p = _hgrn_prompt(hg, hgrn_lb_logits, g_hgrn_out[l], l, _tile(t, 512))
    x1, h2, comb = _postmix(o_nsa, o_h, x_prompt, gt_a[:bp], sc_f[:bp], sh_f[:bp], g_nsa_out[l], g_post_mix[l],
                            g_pre_ffn[l], w_out[l].astype(BF16), w_r, b_r, _tile(t, 512), False)
    wg, wu, wd = _prep_experts(w_exp_gate[l], w_exp_up[l], w_exp_down[l], BF16)
    y_p = _moe(h2, comb, x1, gt_f[:bp], g_post_ffn[l], wg, wu, wd, _tile(t, 1024), False)

    kv_shape = (1, bp, t, KV_GROUPS, 2, HEAD_DIM)
    w_keep = min(WINDOW, t)
    p_win = kv_w[:, t - w_keep:].reshape(1, bp, w_keep, KV_GROUPS, 2, HEAD_DIM)

    n_pool, page = cache_cmp_kv.shape[1], cache_cmp_kv.shape[2]
    wl = cache_win_kv.shape[2]
    past = page_table.shape[1] * page
    assert x_sample.shape[1] == 1 and wl == WINDOW and past % SLC_BLOCK == 0 and page % SLC_BLOCK == 0
    xs = x_sample.reshape(n_dec, d)
    proj = _inproj_sample(xs, sc_a[bp:, 0], sh_a[bp:, 0], g_pre_mix[l][None], w_in[l])
    sizes = [NSA_WIDTH, KV_COLS, KV_COLS, KV_COLS, 3 * NSA_HEADS, HGRN_WIDTH, HGRN_WIDTH, HGRN_WIDTH, HGRN_WIDTH]
    offs = np.cumsum([0] + sizes)
    seg = lambda j: proj[:, offs[j]:offs[j + 1]]
    pad_lanes = lambda a: jnp.pad(a, ((0, 0), (0, 0), (0, LANES - a.shape[2])))
    q8 = pad_lanes(seg(0).reshape(n_dec, NSA_HEADS, HEAD_DIM) * (HEAD_DIM ** -0.5))
    gates8 = pad_lanes(seg(4).reshape(n_dec, NSA_HEADS, 3))
    hg_s = proj[:, offs[5]:offs[9]].reshape(n_dec, 4 * HGRN_HEADS, HGRN_DK)
    tiles = lambda c: jnp.swapaxes(c.reshape(c.shape[0], c.shape[1], KV_COLS), 1, 2)
    oc, ids = _nsa_sample_select(page_table, q8, tiles(cache_cmp_kv[l]), w01, b1r, w2n, w2t)
    o8, s_win_t = _nsa_sample_attend_t(page_table, ids[..., 0].reshape(n_dec, -1), q8, oc, gates8, seg(2)[:, None],
                                       seg(3)[:, None], tiles(cache_win_kv[l]), tiles(cache_slc_kv[l]))
    s_win = jnp.swapaxes(s_win_t, 1, 2)
    o_nsa_s = o8[:, :, HEAD_DIM:].reshape(1, n_dec, NSA_WIDTH)
    oh_s, s_s = _hgrn_sample(hg_s, hgrn_lb_logits, g_hgrn_out[l], state_hgrn[l], l)
    as_row = lambda a: a[bp:].reshape(1, n_dec, d)
    x1s, h2s, comb_s = _postmix(o_nsa_s, oh_s.reshape(1, n_dec, HGRN_WIDTH), xs[None], as_row(gt_a), as_row(sc_f),
                                as_row(sh_f), g_nsa_out[l], g_post_mix[l], g_pre_ffn[l], w_out[l], w_r, b_r,
                                n_dec, True)
    wg32, wu32, wd32 = _prep_experts(w_exp_gate[l], w_exp_up[l], w_exp_down[l], F32)
    y_s = _moe(h2s, comb_s, x1s, as_row(gt_f), g_post_ffn[l], wg32, wu32, wd32, n_dec, True)

    new_shape = (1, n_dec, 1, KV_GROUPS, 2, HEAD_DIM)
    return (y_p, y_s.reshape(n_dec, 1, d), kv_c.reshape(kv_shape), kv_s.reshape(kv_shape), p_win, s_p[None],
            seg(1).reshape(new_shape), seg(2).reshape(new_shape),
            s_win.reshape(1, n_dec, wl, KV_GROUPS, 2, HEAD_DIM), s_s[None])
```

```python
import functools

import numpy as np
import jax
import jax.numpy as jnp
from jax import lax
from jax.experimental import pallas as pl
from jax.experimental.pallas import tpu as pltpu

F32 = jnp.float32
BF16 = jnp.bfloat16
HIGHEST = lax.Precision.HIGHEST

NSA_HEADS = 8
HEAD_DIM = 64
KV_GROUPS = 2
REP = NSA_HEADS // KV_GROUPS
CMP_BLOCK = 32
CMP_STRIDE = 16
SLC_BLOCK = 64
SLC_TOPN = 16
WINDOW = 512
HGRN_HEADS = 4
HGRN_DK = 128
HGRN_DV = 128
HGRN_SUB = 16
N_GROUPS = 4
EXPERTS_PER_GROUP = 4
N_EXPERTS = N_GROUPS * EXPERTS_PER_GROUP
D_EXPERT = 256
NORM_EPS = 1e-6

LANES = 128
KV_COLS = KV_GROUPS * 2 * HEAD_DIM
NSA_WIDTH = NSA_HEADS * HEAD_DIM
HGRN_WIDTH = HGRN_HEADS * HGRN_DV
GATE_ROWS = 16
ONES_ROWS = 16
VT_ROWS = ONES_ROWS + HEAD_DIM
LOG2E = 1.4426950408889634
MASKED = -1e30
M_INIT = -0.5e30
VMEM_LIMIT = 56 * 1024 * 1024

NT_DIMS = (((1,), (1,)), ((), ()))
TN_DIMS = (((0,), (0,)), ((), ()))


def _cparams(*sem):
    return pltpu.CompilerParams(dimension_semantics=sem, vmem_limit_bytes=VMEM_LIMIT)


def _rms(x, g):
    return x * lax.rsqrt(jnp.mean(x * x, axis=-1, keepdims=True) + NORM_EPS) * g


def _silu(x):
    return x * jax.nn.sigmoid(x)


def _dot(a, b, precise):
    if precise:
        return jnp.dot(a, b, preferred_element_type=F32, precision=HIGHEST)
    return jnp.dot(a.astype(BF16), b.astype(BF16), preferred_element_type=F32)


def _dot_nt(a, b, precise=False):
    if precise:
        return lax.dot_general(a, b, NT_DIMS, preferred_element_type=F32, precision=HIGHEST)
    return lax.dot_general(a.astype(BF16), b.astype(BF16), NT_DIMS, preferred_element_type=F32)


def _adaln_kernel(c_ref, w_ref, b_ref, o_ref):
    o_ref[...] = _dot(_silu(c_ref[...]), w_ref[...], True) + b_ref[...]


def _adaln(c, w, b):
    n, d = c.shape
    cols = w.shape[1]
    return pl.pallas_call(
        _adaln_kernel,
        grid=(cols // d,),
        in_specs=[pl.BlockSpec((n, d), lambda j: (0, 0)),
                  pl.BlockSpec((d, d), lambda j: (0, j)),
                  pl.BlockSpec((1, d), lambda j: (0, j))],
        out_specs=pl.BlockSpec((n, d), lambda j: (0, j)),
        out_shape=jax.ShapeDtypeStruct((n, cols), F32),
        compiler_params=_cparams("arbitrary"),
        name="adaln",
    )(c, w, b.reshape(1, cols))


Q_COLS = NSA_HEADS * LANES
HG_COLS = 4 * HGRN_WIDTH
NAT_COLS = Q_COLS + 3 * KV_COLS + HG_COLS
TR_ROWS = 2 * KV_COLS + KV_GROUPS * GATE_ROWS


def _prep_w_in(w_in):
    d = w_in.shape[0]
    sizes = [NSA_WIDTH, KV_COLS, KV_COLS, KV_COLS, 3 * NSA_HEADS, HGRN_WIDTH, HGRN_WIDTH, HGRN_WIDTH, HGRN_WIDTH]
    offs = np.cumsum([0] + sizes)
    wq = w_in[:, offs[0]:offs[1]].reshape(d, NSA_HEADS, HEAD_DIM) * (HEAD_DIM ** -0.5 * LOG2E)
    wq = jnp.pad(wq, ((0, 0), (0, 0), (0, LANES - HEAD_DIM))).reshape(d, Q_COLS)
    w_nat = jnp.concatenate([wq, w_in[:, offs[1]:offs[4]], w_in[:, offs[5]:offs[9]]], axis=1)
    wg = w_in[:, offs[4]:offs[5]].reshape(d, KV_GROUPS, REP, 3).transpose(0, 1, 3, 2)
    wg = jnp.pad(wg.reshape(d, KV_GROUPS, 3 * REP), ((0, 0), (0, 0), (0, GATE_ROWS - 3 * REP)))
    w_tr = jnp.concatenate([w_in[:, offs[2]:offs[3]], w_in[:, offs[3]:offs[4]],
                            wg.reshape(d, KV_GROUPS * GATE_ROWS)], axis=1).T
    return w_nat, w_tr


def _inproj_prompt_kernel(x_ref, sc_ref, sh_ref, g_ref, wn_ref, wt_ref,
                          q_ref, kvc_ref, kvs_ref, kvw_ref, ksb_ref, kwb_ref, hg_ref,
                          vst_ref, vwt_ref, gt_ref):
    h = _rms(x_ref[0], g_ref[...]) * (1.0 + sc_ref[0]) + sh_ref[0]
    hb = h.astype(BF16)

    def nat(lo, hi):
        return jnp.dot(hb, wn_ref[:, lo:hi], preferred_element_type=F32)

    q_ref[0] = nat(0, Q_COLS).astype(BF16)
    c = Q_COLS
    kvc_ref[0] = nat(c, c + KV_COLS)
    kvs = nat(c + KV_COLS, c + 2 * KV_COLS)
    kvs_ref[0] = kvs
    ksb_ref[0] = kvs.astype(BF16)
    kvw = nat(c + 2 * KV_COLS, c + 3 * KV_COLS)
    kvw_ref[0] = kvw
    kwb_ref[0] = kvw.astype(BF16)
    hg_ref[0] = nat(c + 3 * KV_COLS, c + 3 * KV_COLS + HG_COLS)

    tr = lax.dot_general(wt_ref[...], hb, NT_DIMS, preferred_element_type=F32)
    ones = jnp.ones((ONES_ROWS, hb.shape[0]), F32)

    def ones_and_values(kv_t):
        parts = []
        for g in range(KV_GROUPS):
            parts += [ones, kv_t[g * LANES + HEAD_DIM:(g + 1) * LANES]]
        return jnp.concatenate(parts, axis=0).astype(BF16)

    vst_ref[0, 0] = ones_and_values(tr[0:KV_COLS])
    vwt_ref[0, 0] = ones_and_values(tr[KV_COLS:2 * KV_COLS])
    gt_ref[0] = jax.nn.sigmoid(tr[2 * KV_COLS:])


def _inproj_prompt(x, sc, sh, g, w_nat, w_tr, tm):
    b, t, d = x.shape
    nt = t // tm
    row = lambda bi, i: (bi, i, 0)
    const = lambda bi, i: (0, 0)
    out_shapes = [
        jax.ShapeDtypeStruct((b, t, Q_COLS), BF16),
        jax.ShapeDtypeStruct((b, t, KV_COLS), F32),
        jax.ShapeDtypeStruct((b, t, KV_COLS), F32),
        jax.ShapeDtypeStruct((b, t, KV_COLS), F32),
        jax.ShapeDtypeStruct((b, t, KV_COLS), BF16),
        jax.ShapeDtypeStruct((b, t, KV_COLS), BF16),
        jax.ShapeDtypeStruct((b, t, HG_COLS), F32),
        jax.ShapeDtypeStruct((b, nt, KV_GROUPS * VT_ROWS, tm), BF16),
        jax.ShapeDtypeStruct((b, nt, KV_GROUPS * VT_ROWS, tm), BF16),
        jax.ShapeDtypeStruct((b, KV_GROUPS * GATE_ROWS, t), F32),
    ]
    out_specs = [
        pl.BlockSpec((1, tm, Q_COLS), row),
        pl.BlockSpec((1, tm, KV_COLS), row),
        pl.BlockSpec((1, tm, KV_COLS), row),
        pl.BlockSpec((1, tm, KV_COLS), row),
        pl.BlockSpec((1, tm, KV_COLS), row),
        pl.BlockSpec((1, tm, KV_COLS), row),
        pl.BlockSpec((1, tm, HG_COLS), row),
        pl.BlockSpec((1, 1, KV_GROUPS * VT_ROWS, tm), lambda bi, i: (bi, i, 0, 0)),
        pl.BlockSpec((1, 1, KV_GROUPS * VT_ROWS, tm), lambda bi, i: (bi, i, 0, 0)),
        pl.BlockSpec((1, KV_GROUPS * GATE_ROWS, tm), lambda bi, i: (bi, 0, i)),
    ]
    return pl.pallas_call(
        _inproj_prompt_kernel,
        grid=(b, nt),
        in_specs=[pl.BlockSpec((1, tm, d), row),
                  pl.BlockSpec((1, 1, d), lambda bi, i: (bi, 0, 0)),
                  pl.BlockSpec((1, 1, d), lambda bi, i: (bi, 0, 0)),
                  pl.BlockSpec((1, d), const),
                  pl.BlockSpec((d, NAT_COLS), const),
                  pl.BlockSpec((TR_ROWS, d), const)],
        out_specs=out_specs,
        out_shape=out_shapes,
        compiler_params=_cparams("parallel", "parallel"),
        name="inproj_prompt",
    )(x, sc, sh, g, w_nat, w_tr)


def _prep_compress(w1, b1, w2):
    ratio = CMP_BLOCK // CMP_STRIDE
    w1r = w1.reshape(2, ratio, CMP_STRIDE, HEAD_DIM, HEAD_DIM)
    eye_k = jnp.eye(2, dtype=F32)
    w01 = jnp.einsum('krsde,kl->skdrle', w1r, eye_k).reshape(CMP_STRIDE, LANES, ratio * LANES)
    w2n = jnp.einsum('kef,kl->kelf', w2, eye_k).reshape(LANES, LANES)
    return w01.astype(BF16), b1.reshape(1, LANES), w2n.astype(BF16), w2n.T.astype(BF16)


def _gelu_tanh(x):
    return 0.5 * x * (1.0 + jnp.tanh(0.7978845608028654 * (x + 0.044715 * (x * x * x))))


def _compress_group(load_rows, n_chunks, w01_ref, b1_ref, w2n_ref, w2t_ref):
    y = jnp.zeros((n_chunks, 2 * LANES), F32)
    for s in range(CMP_STRIDE):
        y = y + jnp.dot(load_rows(s).astype(BF16), w01_ref[s], preferred_element_type=F32)
    nxt = pltpu.roll(y[:, LANES:], n_chunks - 1, 0)
    hb = _gelu_tanh(y[:, :LANES] + nxt + b1_ref[...]).astype(BF16)
    return (jnp.dot(hb, w2n_ref[...], preferred_element_type=F32),
            lax.dot_general(w2t_ref[...], hb, NT_DIMS, preferred_element_type=F32))


def _compress_prompt_kernel(*refs):
    x_refs = refs[:KV_GROUPS]
    w01_ref, b1_ref, w2n_ref, w2t_ref, kc_ref, kct_ref = refs[KV_GROUPS:]
    n_chunks = kc_ref.shape[1]
    for g in range(KV_GROUPS):
        load = lambda s, g=g: x_refs[g][0, pl.ds(s, n_chunks, stride=CMP_STRIDE), :]
        kc, kct = _compress_group(load, n_chunks, w01_ref, b1_ref, w2n_ref, w2t_ref)
        kc_ref[0, :, g * LANES:(g + 1) * LANES] = kc.astype(BF16)
        kct_ref[0, g * LANES:(g + 1) * LANES, :] = kct.astype(BF16)


def _compress_prompt(kv_c, w01, b1r, w2n, w2t):
    b, t, _ = kv_c.shape
    n_chunks = t // CMP_STRIDE
    const2 = lambda i: (0, 0)
    return pl.pallas_call(
        _compress_prompt_kernel,
        grid=(b,),
        in_specs=[pl.BlockSpec((1, t, LANES), lambda i, g=g: (i, 0, g)) for g in range(KV_GROUPS)]
        + [pl.BlockSpec(w01.shape, lambda i: (0, 0, 0)),
           pl.BlockSpec((1, LANES), const2),
           pl.BlockSpec((LANES, LANES), const2),
           pl.BlockSpec((LANES, LANES), const2)],
        out_specs=[pl.BlockSpec((1, n_chunks, KV_COLS), lambda i: (i, 0, 0)),
                   pl.BlockSpec((1, KV_COLS, n_chunks), lambda i: (i, 0, 0))],
        out_shape=[jax.ShapeDtypeStruct((b, n_chunks, KV_COLS), BF16),
                   jax.ShapeDtypeStruct((b, KV_COLS, n_chunks), BF16)],
        compiler_params=_cparams("parallel"),
        name="compress_prompt",
    )(*([kv_c] * KV_GROUPS), w01, b1r, w2n, w2t)


def _overlap_t(n_chunks, n_slc):
    start = np.arange(n_chunks)[None, :] * CMP_STRIDE
    j = np.arange(n_slc)[:, None]
    return ((start < (j + 1) * SLC_BLOCK) & (start + CMP_BLOCK > j * SLC_BLOCK)).astype(np.float32)


def _split3(x):
    hi = x.astype(BF16)
    r1 = x - hi.astype(F32)
    mid = r1.astype(BF16)
    lo = (r1 - mid.astype(F32)).astype(BF16)
    return hi, mid, lo


RANK_CHUNK = 16


def _rank_select_bias(cnt_ref, score, n_rows, topn):
    n = score.shape[0]
    j = lax.broadcasted_iota(jnp.int32, score.shape, 0)
    cnt_ref[...] = jnp.zeros(score.shape, F32)
    for lo in range(0, n, RANK_CHUNK):
        @pl.when(lo < n_rows)
        def _(lo=lo):
            cnt = cnt_ref[...]
            for jp in range(lo, min(lo + RANK_CHUNK, n)):
                row = score[jp:jp + 1, :]
                beats = (row > score) | ((row == score) & (j > jp))
                cnt = cnt + beats.astype(F32)
            cnt_ref[...] = cnt
    return jnp.where(cnt_ref[...] < topn, 0.0, MASKED)


def _nsa_prompt_kernel(slopes_ref, q_ref, kc_ref, kct_ref, ks_ref, vst_ref, kw_ref, vwt_ref, gt_ref, ovt_ref,
                       o_ref, sel_ref, m_ref, acc_ref, sc_ref, s0_ref, s1_ref, w0_ref, w1_ref, bias_ref, cbias_ref):
    g = pl.program_id(1)
    i = pl.program_id(2)
    tq = q_ref.shape[1]
    kt = vst_ref.shape[3]
    wide = REP * tq
    n_chunks = kc_ref.shape[1]
    n_slc = ovt_ref.shape[0]
    blocks_per_tile = kt // SLC_BLOCK
    win_tiles = WINDOW // kt
    q0 = i * tq
    qpos = q0 + lax.broadcasted_iota(jnp.int32, (1, tq), 1)
    qpos_f = qpos.astype(F32)
    per_head = lambda x: jnp.concatenate([x] * REP, axis=1)
    lane_head = lax.broadcasted_iota(jnp.int32, (1, wide), 1) // tq
    slope_row = jnp.zeros((1, wide), F32)
    for r in range(REP):
        slope_row = jnp.where(lane_head == r, slopes_ref[g * REP + r], slope_row)
    q_all = jnp.concatenate([q_ref[0, :, r * LANES:(r + 1) * LANES] for r in range(REP)], axis=0)
    e_pos = (lax.broadcasted_iota(jnp.int32, (n_chunks, 1), 0) * CMP_STRIDE + (CMP_BLOCK - 1)).astype(F32)

    def scores(k_ref, tile, dst_ref):
        keys = k_ref[0, pl.ds(pl.multiple_of(tile * kt, kt), kt), :]
        dst_ref[...] = lax.dot_general(keys, q_all, NT_DIMS, preferred_element_type=F32)

    first_win = jnp.maximum(i - win_tiles, 0)
    sc_ref[...] = lax.dot_general(kc_ref[0], q_all, NT_DIMS, preferred_element_type=F32)
    scores(ks_ref, 0, s0_ref)
    scores(kw_ref, first_win, w0_ref)

    @pl.when(i == 0)
    def _():
        col = lax.broadcasted_iota(jnp.int32, (1, wide), 1) % tq
        d0 = (lax.broadcasted_iota(jnp.int32, (kt, 1), 0) - col).astype(F32)
        base = slope_row * d0
        bias_ref[0] = jnp.where(d0 > 0.0, base, MASKED)
        bias_ref[1] = base
        bias_ref[2] = jnp.where(d0 <= 0.0, base, MASKED)
        cbias_ref[...] = slope_row * (e_pos - col.astype(F32))

    u = sc_ref[...] + cbias_ref[...] + per_head(jnp.where(e_pos <= qpos_f, 0.0, MASKED))
    m = jnp.max(u, axis=0, keepdims=True)
    m = jnp.where(m > M_INIT, m, 0.0)
    e = jnp.exp2(u - m)
    p = e * (1.0 / jnp.maximum(jnp.sum(e, axis=0, keepdims=True), 1e-30))
    o_cmp = jnp.dot(kct_ref[0], p.astype(BF16), preferred_element_type=F32)[HEAD_DIM:]
    p_grp = sum(p[:, r * tq:(r + 1) * tq] for r in range(REP))
    ovt = ovt_ref[...]
    p_slc = sum(jnp.dot(ovt, part, preferred_element_type=F32) for part in _split3(p_grp))
    j = lax.broadcasted_iota(jnp.int32, (n_slc, 1), 0)
    cur = qpos // SLC_BLOCK
    forced = (j == 0) | (j == cur) | (j == cur - 1)
    valid_j = (j * SLC_BLOCK) <= qpos
    score = jnp.where(forced, jnp.inf, jnp.where(valid_j, p_slc, -jnp.inf))
    n_rows = (q0 + tq + SLC_BLOCK - 1) // SLC_BLOCK
    sel_ref[...] = _rank_select_bias(sel_ref, score, n_rows, min(SLC_TOPN, n_slc))

    def branch(k_ref, vt_ref, first, use_sel, cur0_ref, cur1_ref):
        m_ref[...] = jnp.full(m_ref.shape, M_INIT, F32)
        acc_ref[...] = jnp.zeros(acc_ref.shape, F32)

        def step(t, cur_ref, nxt_ref):
            if nxt_ref is not None:
                scores(k_ref, t + 1, nxt_ref)
            behind = i - t
            if use_sel:
                variant = jnp.where(behind == 0, 2, 1)
            else:
                variant = jnp.where(behind == 0, 2, jnp.where(behind == win_tiles, 0, 1))
            u = cur_ref[...] + bias_ref[variant]
            if use_sel:
                u = u + per_head(jnp.concatenate(
                    [jnp.broadcast_to(sel_ref[pl.ds(t * blocks_per_tile + b, 1), :], (SLC_BLOCK, tq))
                     for b in range(blocks_per_tile)], axis=0))
            c = slope_row * (-(behind * kt)).astype(F32)
            m_old = m_ref[0:1, :]
            m_new = jnp.maximum(m_old, jnp.max(u, axis=0, keepdims=True) + c)
            p = jnp.exp2(u - (m_new - c))
            m_ref[0:1, :] = m_new
            acc_ref[...] = (acc_ref[...] * jnp.exp2(m_old - m_new)
                            + jnp.dot(vt_ref[0, t], p.astype(BF16), preferred_element_type=F32))

        n_tiles = i - first + 1

        def pair(pi, carry):
            t = first + 2 * pi
            step(t, cur0_ref, cur1_ref)
            step(t + 1, cur1_ref, cur0_ref)
            return carry

        lax.fori_loop(0, (n_tiles - 1) // 2, pair, 0)

        @pl.when(n_tiles % 2 == 1)
        def _():
            step(i, cur0_ref, None)

        @pl.when(n_tiles % 2 == 0)
        def _():
            step(i - 1, cur0_ref, cur1_ref)
            step(i, cur1_ref, None)

        return acc_ref[ONES_ROWS:, :] * (1.0 / jnp.maximum(acc_ref[0:1, :], 1e-30))

    o_slc = branch(ks_ref, vst_ref, 0, True, s0_ref, s1_ref)
    o_win = branch(kw_ref, vwt_ref, first_win, False, w0_ref, w1_ref)

    gates = gt_ref[0]
    gate = lambda br: jnp.concatenate([gates[br * REP + r:br * REP + r + 1] for r in range(REP)], axis=1)
    out = gate(0) * o_cmp + gate(1) * o_slc + gate(2) * o_win
    o_ref[0] = jnp.concatenate([out[:, r * tq:(r + 1) * tq] for r in range(REP)], axis=0).T


def _nsa_prompt(q, kc, kct, ksb, vst, kwb, vwt, gates_t, tq):
    b, t, _ = q.shape
    n_chunks = kc.shape[1]
    n_tiles, kt = vst.shape[1], vst.shape[3]
    n_slc = -(-t // SLC_BLOCK)
    wide = REP * tq
    slopes = LOG2E * jnp.exp2(-8.0 * jnp.arange(1, NSA_HEADS + 1, dtype=F32) / NSA_HEADS)
    ovt = jnp.asarray(_overlap_t(n_chunks, n_slc), BF16)
    grp = lambda bi, gi, i, s: (bi, 0, gi)
    grid_spec = pltpu.PrefetchScalarGridSpec(
        num_scalar_prefetch=1,
        grid=(b, KV_GROUPS, t // tq),
        in_specs=[pl.BlockSpec((1, tq, REP * LANES), lambda bi, gi, i, s: (bi, i, gi)),
                  pl.BlockSpec((1, n_chunks, LANES), grp),
                  pl.BlockSpec((1, LANES, n_chunks), lambda bi, gi, i, s: (bi, gi, 0)),
                  pl.BlockSpec((1, t, LANES), grp),
                  pl.BlockSpec((1, n_tiles, VT_ROWS, kt), lambda bi, gi, i, s: (bi, 0, gi, 0)),
                  pl.BlockSpec((1, t, LANES), grp),
                  pl.BlockSpec((1, n_tiles, VT_ROWS, kt), lambda bi, gi, i, s: (bi, 0, gi, 0)),
                  pl.BlockSpec((1, GATE_ROWS, tq), lambda bi, gi, i, s: (bi, gi, i)),
                  pl.BlockSpec((n_slc, n_chunks), lambda bi, gi, i, s: (0, 0))],
        out_specs=pl.BlockSpec((1, tq, REP * HEAD_DIM), lambda bi, gi, i, s: (bi, i, gi)),
        scratch_shapes=[pltpu.VMEM((n_slc, tq), F32),
                        pltpu.VMEM((8, wide), F32),
                        pltpu.VMEM((VT_ROWS, wide), F32),
                        pltpu.VMEM((n_chunks, wide), F32),
                        pltpu.VMEM((kt, wide), F32),
                        pltpu.VMEM((kt, wide), F32),
                        pltpu.VMEM((kt, wide), F32),
                        pltpu.VMEM((kt, wide), F32),
                        pltpu.VMEM((3, kt, wide), F32),
                        pltpu.VMEM((n_chunks, wide), F32)],
    )
    assert kt == tq and WINDOW % kt == 0 and t % tq == 0 and n_slc % 4 == 0
    return pl.pallas_call(
        _nsa_prompt_kernel,
        grid_spec=grid_spec,
        out_shape=jax.ShapeDtypeStruct((b, t, NSA_WIDTH), F32),
        compiler_params=_cparams("parallel", "parallel", "arbitrary"),
        name="nsa_prompt",
    )(slopes, q, kc, kct, ksb, vst, kwb, vwt, gates_t, ovt)


def _lower_bound(lb_ref, layer):
    x = lb_ref[...]
    e = jnp.exp(x - jnp.max(x, axis=0, keepdims=True))
    return jnp.sum(e[0:layer + 1], axis=0, keepdims=True) / jnp.sum(e, axis=0, keepdims=True)


def _row_to_col(row):
    n = row.shape[1]
    eye = lax.broadcasted_iota(jnp.int32, (n, n), 0) == lax.broadcasted_iota(jnp.int32, (n, n), 1)
    return jnp.sum(jnp.where(eye, jnp.broadcast_to(row, (n, n)), 0.0), axis=1, keepdims=True)


def _cumsum_rows(x):
    n = x.shape[0]
    row = lax.broadcasted_iota(jnp.int32, (n, 1), 0)
    shift = 1
    while shift < n:
        x = x + jnp.where(row >= shift, pltpu.roll(x, shift, 0), 0.0)
        shift *= 2
    return x


def _hgrn_prompt_kernel(layer, q_ref, f_ref, i_ref, og_ref, lb_ref, g_ref, o_ref, s_ref, state):
    t_idx = pl.program_id(2)
    blk = HGRN_DK
    n_blk = q_ref.shape[1] // blk
    n_sub = blk // HGRN_SUB

    @pl.when(t_idx == 0)
    def _():
        state[...] = jnp.zeros(state.shape, F32)

    lb = _lower_bound(lb_ref, layer)
    gain = g_ref[0]
    row = lax.broadcasted_iota(jnp.int32, (blk, 1), 0)
    causal = lax.broadcasted_iota(jnp.int32, (blk, blk), 1) <= lax.broadcasted_iota(jnp.int32, (blk, blk), 0)

    s0 = state[...]
    for bi in range(n_blk):
        rows = pl.ds(bi * blk, blk)
        fz = f_ref[0, rows, :]
        q = _silu(q_ref[0, rows, :])
        k = (1.0 - lb) * jax.nn.sigmoid(-fz)
        v = i_ref[0, rows, :].astype(BF16)
        cum = _cumsum_rows(jnp.log(lb + (1.0 - lb) * jax.nn.sigmoid(fz)))
        a_rows = []
        for sub in range(n_sub):
            lo, hi = sub * HGRN_SUB, (sub + 1) * HGRN_SUB
            ref_row = cum[lo - 1:lo] if sub else jnp.zeros((1, blk), F32)
            qe = q[lo:hi] * jnp.exp(cum[lo:hi] - ref_row)
            ke = k * jnp.exp(jnp.where(row < hi, ref_row - cum, MASKED))
            a_rows.append(_dot_nt(qe, ke))
        a = jnp.where(causal, jnp.concatenate(a_rows, axis=0), 0.0)
        o = (jnp.dot(a.astype(BF16), v, preferred_element_type=F32)
             + _dot(q * jnp.exp(cum), s0, False))
        last = cum[blk - 1:blk]
        kd = (k * jnp.exp(last - cum)).astype(BF16)
        s0 = _row_to_col(jnp.exp(last)) * s0 + lax.dot_general(kd, v, TN_DIMS, preferred_element_type=F32)
        o_ref[0, rows, :] = (_rms(o, gain) * _silu(og_ref[0, rows, :])).astype(o_ref.dtype)
    state[...] = s0

    @pl.when(t_idx == pl.num_programs(2) - 1)
    def _():
        s_ref[0, 0] = state[...]


def _hgrn_prompt(hg, lb_logits, g_out, layer, tc):
    b, t, _ = hg.shape
    part = lambda p: pl.BlockSpec((1, tc, HGRN_DK), lambda bi, h, i, p=p: (bi, i, p * HGRN_HEADS + h))
    n_layers = lb_logits.shape[0]
    return pl.pallas_call(
        functools.partial(_hgrn_prompt_kernel, layer),
        grid=(b, HGRN_HEADS, t // tc),
        in_specs=[part(0), part(1), part(2), part(3),
                  pl.BlockSpec((n_layers, HGRN_DK), lambda bi, h, i: (0, h)),
                  pl.BlockSpec((1, 1, HGRN_DV), lambda bi, h, i: (h, 0, 0))],
        out_specs=[pl.BlockSpec((1, tc, HGRN_DV), lambda bi, h, i: (bi, i, h)),
                   pl.BlockSpec((1, 1, HGRN_DK, HGRN_DV), lambda bi, h, i: (bi, h, 0, 0))],
        out_shape=[jax.ShapeDtypeStruct((b, t, HGRN_WIDTH), BF16),
                   jax.ShapeDtypeStruct((b, HGRN_HEADS, HGRN_DK, HGRN_DV), F32)],
        scratch_shapes=[pltpu.VMEM((HGRN_DK, HGRN_DV), F32)],
        compiler_params=_cparams("parallel", "parallel", "arbitrary"),
        name="hgrn_prompt",
    )(hg, hg, hg, hg, lb_logits, g_out.reshape(HGRN_HEADS, 1, HGRN_DV))


ROUTE_LANES = LANES
EXPERT_LANE0 = N_GROUPS


def _prep_router(w_rg, b_rg, w_re, b_re):
    d = w_rg.shape[0]
    pad = ROUTE_LANES - N_GROUPS - N_EXPERTS
    w = jnp.concatenate([w_rg, w_re, jnp.zeros((d, pad), F32)], axis=1)
    b = jnp.concatenate([b_rg, b_re, jnp.zeros((pad,), F32)]).reshape(1, ROUTE_LANES)
    return w, b


def _first_lane_of_max(x, lane):
    m = jnp.max(x, axis=1, keepdims=True)
    return m, jnp.min(jnp.where(x == m, lane, float(ROUTE_LANES)), axis=1, keepdims=True)


def _route(logits):
    lane = lax.broadcasted_iota(jnp.int32, (1, ROUTE_LANES), 1).astype(F32)
    is_grp = lane < N_GROUPS
    is_exp = (lane >= EXPERT_LANE0) & (lane < EXPERT_LANE0 + N_EXPERTS)
    lg = jnp.where(is_grp, logits, -jnp.inf)
    mg, g_star = _first_lane_of_max(lg, lane)
    pg_top = 1.0 / jnp.sum(jnp.exp(lg - mg), axis=1, keepdims=True)
    exp_grp = jnp.floor((lane - EXPERT_LANE0) / EXPERTS_PER_GROUP)
    le = jnp.where(is_exp & (exp_grp == g_star), logits, -jnp.inf)
    v1, i1 = _first_lane_of_max(le, lane)
    v2, i2 = _first_lane_of_max(jnp.where(lane == i1, -jnp.inf, le), lane)
    e2 = jnp.exp(v2 - v1)
    w1 = pg_top / (1.0 + e2)
    return jnp.where(lane == i1, w1, 0.0) + jnp.where(lane == i2, w1 * e2, 0.0)


def _postmix_kernel(precise, on_ref, oh_ref, x_ref, gt_ref, sc_ref, sh_ref, gn_ref, gp_ref, gf_ref,
                    wo_ref, wr_ref, br_ref, x1_ref, h2_ref, comb_ref):
    on = _rms(on_ref[0], gn_ref[...])
    mixed = _dot(on, wo_ref[0:NSA_WIDTH, :], precise) + _dot(oh_ref[0], wo_ref[NSA_WIDTH:, :], precise)
    x1 = x_ref[0] + gt_ref[0] * _rms(mixed, gp_ref[...])
    h2 = _rms(x1, gf_ref[...]) * (1.0 + sc_ref[0]) + sh_ref[0]
    x1_ref[0] = x1
    h2_ref[0] = h2.astype(h2_ref.dtype)
    comb_ref[0] = _route(_dot(h2, wr_ref[...], precise) + br_ref[...])


def _postmix(o_nsa, o_h, x, gt, sc, sh, g_nsa, g_post, g_pre_ffn, w_out, w_r, b_r, tm, precise):
    b, t, d = x.shape
    row = lambda bi, i: (bi, i, 0)
    const = lambda bi, i: (0, 0)
    mod = (pl.BlockSpec((1, 1, d), lambda bi, i: (bi, 0, 0)) if gt.shape[1] == 1
           else pl.BlockSpec((1, tm, d), row))
    vec = lambda n: pl.BlockSpec((1, n), const)
    return pl.pallas_call(
        functools.partial(_postmix_kernel, precise),
        grid=(b, t // tm),
        in_specs=[pl.BlockSpec((1, tm, NSA_WIDTH), row), pl.BlockSpec((1, tm, HGRN_WIDTH), row),
                  pl.BlockSpec((1, tm, d), row), mod, mod, mod,
                  vec(NSA_WIDTH), vec(d), vec(d),
                  pl.BlockSpec(w_out.shape, const), pl.BlockSpec(w_r.shape, const), vec(ROUTE_LANES)],
        out_specs=[pl.BlockSpec((1, tm, d), row), pl.BlockSpec((1, tm, d), row),
                   pl.BlockSpec((1, tm, ROUTE_LANES), row)],
        out_shape=[jax.ShapeDtypeStruct((b, t, d), F32),
                   jax.ShapeDtypeStruct((b, t, d), F32 if precise else BF16),
                   jax.ShapeDtypeStruct((b, t, ROUTE_LANES), F32)],
        compiler_params=_cparams("parallel", "parallel"),
        name="postmix",
    )(o_nsa, o_h, x, gt, sc, sh, g_nsa.reshape(1, -1), g_post.reshape(1, -1), g_pre_ffn.reshape(1, -1),
      w_out, w_r, b_r)


def _prep_experts(w_gate, w_up, w_down, dtype):
    e, d, f = w_gate.shape
    side = lambda w: w.reshape(N_GROUPS, EXPERTS_PER_GROUP, d, f).transpose(0, 2, 1, 3).reshape(
        N_GROUPS, d, EXPERTS_PER_GROUP * f).astype(dtype)
    return side(w_gate), side(w_up), w_down.reshape(N_GROUPS, EXPERTS_PER_GROUP * f, d).astype(dtype)


def _moe_kernel(precise, h_ref, comb_ref, x1_ref, gt_ref, gp_ref, wg_ref, wu_ref, wd_ref, y_ref, acc_ref):
    grp = pl.program_id(2)

    @pl.when(grp == 0)
    def _():
        acc_ref[...] = jnp.zeros(acc_ref.shape, F32)

    h = h_ref[0]
    act = _silu(_dot(h, wg_ref[0], precise)) * _dot(h, wu_ref[0], precise)
    lane = lax.broadcasted_iota(jnp.int32, (1, ROUTE_LANES), 1)
    comb = comb_ref[0]
    parts = []
    for e in range(EXPERTS_PER_GROUP):
        sel = lane == EXPERT_LANE0 + grp * EXPERTS_PER_GROUP + e
        cw = jnp.sum(jnp.where(sel, comb, 0.0), axis=1, keepdims=True)
        parts.append(act[:, e * D_EXPERT:(e + 1) * D_EXPERT] * cw)
    acc_ref[...] += _dot(jnp.concatenate(parts, axis=1), wd_ref[0], precise)

    @pl.when(grp == pl.num_programs(2) - 1)
    def _():
        y_ref[0] = x1_ref[0] + gt_ref[0] * _rms(acc_ref[...], gp_ref[...])


def _moe(h2, comb, x1, gt, g_post, wg, wu, wd, tm, precise):
    b, t, d = x1.shape
    row = lambda bi, i, g: (bi, i, 0)
    mod = (pl.BlockSpec((1, 1, d), lambda bi, i, g: (bi, 0, 0)) if gt.shape[1] == 1
           else pl.BlockSpec((1, tm, d), row))
    wide = EXPERTS_PER_GROUP * D_EXPERT
    return pl.pallas_call(
        functools.partial(_moe_kernel, precise),
        grid=(b, t // tm, N_GROUPS),
        in_specs=[pl.BlockSpec((1, tm, d), row), pl.BlockSpec((1, tm, ROUTE_LANES), row),
                  pl.BlockSpec((1, tm, d), row), mod,
                  pl.BlockSpec((1, d), lambda bi, i, g: (0, 0)),
                  pl.BlockSpec((1, d, wide), lambda bi, i, g: (g, 0, 0)),
                  pl.BlockSpec((1, d, wide), lambda bi, i, g: (g, 0, 0)),
                  pl.BlockSpec((1, wide, d), lambda bi, i, g: (g, 0, 0))],
        out_specs=pl.BlockSpec((1, tm, d), row),
        out_shape=jax.ShapeDtypeStruct((b, t, d), F32),
        scratch_shapes=[pltpu.VMEM((tm, d), F32)],
        compiler_params=_cparams("parallel", "parallel", "arbitrary"),
        name="moe",
    )(h2, comb, x1, gt, g_post.reshape(1, -1), wg, wu, wd)


def _inproj_sample_kernel(x_ref, sc_ref, sh_ref, g_ref, w_ref, o_ref):
    h = _rms(x_ref[...], g_ref[...]) * (1.0 + sc_ref[...]) + sh_ref[...]
    o_ref[...] = _dot(h, w_ref[...], True)


def _inproj_sample(x, sc, sh, g, w):
    n, d = x.shape
    cols = w.shape[1]
    full = lambda shape: pl.BlockSpec(shape, lambda i: (0, 0))
    return pl.pallas_call(
        _inproj_sample_kernel,
        grid=(1,),
        in_specs=[full((n, d)), full((n, d)), full((n, d)), full((1, d)), full((d, cols))],
        out_specs=full((n, cols)),
        out_shape=jax.ShapeDtypeStruct((n, cols), F32),
        compiler_params=_cparams("arbitrary"),
        name="inproj_sample",
    )(x, sc, sh, g, w)


def _head_consts():
    row = lax.broadcasted_iota(jnp.int32, (NSA_HEADS, 1), 0)
    slope = jnp.exp2(-(row + 1).astype(F32) * (8.0 / NSA_HEADS))
    return row // REP, slope


def _masked_softmax_rows(s, slope, dist, valid):
    logits = jnp.where(valid, s - slope * dist, MASKED)
    m = jnp.max(logits, axis=1, keepdims=True)
    m = jnp.where(m > M_INIT, m, 0.0)
    e = jnp.exp(logits - m)
    return e / jnp.maximum(jnp.sum(e, axis=1, keepdims=True), 1e-30)


def _dot_split(a, b, nt=False):
    a_hi, a_mid, a_lo = _split3(a)
    b_hi = b.astype(BF16)
    b_lo = (b - b_hi.astype(F32)).astype(BF16)
    n = a.shape[0]
    dims = NT_DIMS if nt else (((1,), (0,)), ((), ()))
    first = lax.dot_general(jnp.concatenate([a_hi, a_mid, a_lo], axis=0), b_hi, dims, preferred_element_type=F32)
    second = lax.dot_general(jnp.concatenate([a_hi, a_mid], axis=0), b_lo, dims, preferred_element_type=F32)
    return first[0:n] + first[n:2 * n] + first[2 * n:3 * n] + second[0:n] + second[n:2 * n]


def _page_copies(pool_ref, pt_ref, xt_buf, sem, b, slot, wait):
    def body(p, carry):
        cp = pltpu.make_async_copy(pool_ref.at[pt_ref[b, p]], xt_buf.at[slot, p], sem.at[slot])
        if wait:
            cp.wait()
        else:
            cp.start()
        return carry

    lax.fori_loop(0, pt_ref.shape[1], body, 0)


def _nsa_sample_select_kernel(pt_ref, q_ref, pool_ref, w01_ref, b1_ref, w2n_ref, w2t_ref, ov_ref,
                              oc_ref, ids_ref, xt_buf, x_buf, sem):
    b = pl.program_id(0)
    slot = b % 2
    n_pages, page = pt_ref.shape[1], pool_ref.shape[2]
    past = n_pages * page
    n_chunks = past // CMP_STRIDE
    n_slc = -(-(past + 1) // SLC_BLOCK)
    nsp = ov_ref.shape[1]
    topn = ids_ref.shape[2]

    @pl.when(b == 0)
    def _():
        _page_copies(pool_ref, pt_ref, xt_buf, sem, b, slot, False)

    @pl.when(b + 1 < pl.num_programs(0))
    def _():
        _page_copies(pool_ref, pt_ref, xt_buf, sem, b + 1, 1 - slot, False)

    _page_copies(pool_ref, pt_ref, xt_buf, sem, b, slot, True)

    pages_per_iter = 8 if n_pages % 8 == 0 else 1

    def to_rows(pi, carry):
        for d in range(pages_per_iter):
            p = pi * pages_per_iter + d
            for g in range(KV_GROUPS):
                x_buf[g, pl.ds(pl.multiple_of(p * page, page), page), :] = \
                    xt_buf[slot, p, g * LANES:(g + 1) * LANES, :].T
        return carry

    lax.fori_loop(0, n_pages // pages_per_iter, to_rows, 0)

    q8 = q_ref[0]
    grp_of_row, slope = _head_consts()
    e_pos = lax.broadcasted_iota(jnp.int32, (1, n_chunks), 1) * CMP_STRIDE + (CMP_BLOCK - 1)
    dist = (past - e_pos).astype(F32)
    kcs = []
    s = jnp.zeros((NSA_HEADS, n_chunks), F32)
    for g in range(KV_GROUPS):
        load = lambda sub, g=g: x_buf[g, pl.ds(sub, n_chunks, stride=CMP_STRIDE), :]
        kc, _ = _compress_group(load, n_chunks, w01_ref, b1_ref, w2n_ref, w2t_ref)
        kcs.append(kc)
        s = jnp.where(grp_of_row == g, _dot_split(q8, kc, nt=True), s)
    p = _masked_softmax_rows(s, slope, dist, dist >= 0.0)
    oc = jnp.zeros((NSA_HEADS, LANES), F32)
    for g in range(KV_GROUPS):
        oc = jnp.where(grp_of_row == g, _dot_split(p, kcs[g]), oc)
    oc_ref[0] = oc

    p_grp = jnp.concatenate([jnp.sum(p[g * REP:(g + 1) * REP], axis=0, keepdims=True) for g in range(KV_GROUPS)]
                            + [jnp.zeros((NSA_HEADS - KV_GROUPS, n_chunks), F32)], axis=0)
    ov = ov_ref[...]
    p_slc = sum(jnp.dot(part, ov, preferred_element_type=F32) for part in _split3(p_grp))
    jl = lax.broadcasted_iota(jnp.int32, (1, nsp), 1)
    jp = lax.broadcasted_iota(jnp.int32, (nsp, 1), 0)
    cur = past // SLC_BLOCK
    forced = (jl == 0) | (jl == cur) | (jl == cur - 1)
    valid_j = (jl * SLC_BLOCK <= past) & (jl < n_slc)
    score = jnp.where(forced, jnp.inf, jnp.where(valid_j, p_slc, -jnp.inf))
    k_col = lax.broadcasted_iota(jnp.int32, (topn, 1), 0).astype(F32)
    for g in range(KV_GROUPS):
        row_s = score[g:g + 1]
        col_s = _row_to_col(row_s)
        beats = (col_s > row_s) | ((col_s == row_s) & (jp < jl))
        sel = (jnp.sum(beats.astype(F32), axis=0, keepdims=True) < topn).astype(F32)
        before = jnp.sum(jnp.where(jp < jl, _row_to_col(sel), 0.0), axis=0, keepdims=True)
        onehot = jnp.where((before == k_col) & (sel > 0.5), 1.0, 0.0)
        idx = jnp.sum(onehot * jl.astype(F32), axis=1, keepdims=True)
        ids_ref[0, g] = jnp.broadcast_to(idx, (topn, LANES)).astype(jnp.int32)


def _nsa_sample_select(page_table, q8, pool_c, w01, b1r, w2n, w2t):
    n_dec, n_pages = page_table.shape
    page = pool_c.shape[2]
    past = n_pages * page
    n_chunks = past // CMP_STRIDE
    n_slc = -(-(past + 1) // SLC_BLOCK)
    nsp = -(-n_slc // LANES) * LANES
    topn = min(SLC_TOPN, n_slc)
    ov = np.zeros((n_chunks, nsp), np.float32)
    ov[:, :n_slc] = _overlap_t(n_chunks, n_slc).T
    const2 = lambda i, pt: (0, 0)
    grid_spec = pltpu.PrefetchScalarGridSpec(
        num_scalar_prefetch=1,
        grid=(n_dec,),
        in_specs=[pl.BlockSpec((1, NSA_HEADS, LANES), lambda i, pt: (i, 0, 0)),
                  pl.BlockSpec(memory_space=pl.ANY),
                  pl.BlockSpec(w01.shape, lambda i, pt: (0, 0, 0)),
                  pl.BlockSpec((1, LANES), const2),
                  pl.BlockSpec((LANES, LANES), const2),
                  pl.BlockSpec((LANES, LANES), const2),
                  pl.BlockSpec((n_chunks, nsp), const2)],
        out_specs=[pl.BlockSpec((1, NSA_HEADS, LANES), lambda i, pt: (i, 0, 0)),
                   pl.BlockSpec((1, KV_GROUPS, topn, LANES), lambda i, pt: (i, 0, 0, 0))],
        scratch_shapes=[pltpu.VMEM((2, n_pages, KV_COLS, page), F32),
                        pltpu.VMEM((KV_GROUPS, past, LANES), F32),
                        pltpu.SemaphoreType.DMA((2,))],
    )
    return pl.pallas_call(
        _nsa_sample_select_kernel,
        grid_spec=grid_spec,
        out_shape=[jax.ShapeDtypeStruct((n_dec, NSA_HEADS, LANES), F32),
                   jax.ShapeDtypeStruct((n_dec, KV_GROUPS, topn, LANES), jnp.int32)],
        compiler_params=_cparams("arbitrary"),
        name="nsa_sample_select",
    )(page_table, q8, pool_c, w01, b1r, w2n, w2t, jnp.asarray(ov, BF16))


def _page_tile_copies(pool_ref, pt_ref, ids_ref, s_buf, sem, b, slot, n_past_blocks, wait):
    topn = ids_ref.shape[1] // KV_GROUPS
    per_page = pool_ref.shape[2] // SLC_BLOCK
    for g in range(KV_GROUPS):
        for k in range(topn):
            j = ids_ref[b, g * topn + k]

            @pl.when(j < n_past_blocks)
            def _(g=g, k=k, j=j):
                cp = pltpu.make_async_copy(pool_ref.at[pt_ref[b, j // per_page], pl.ds(g * LANES, LANES), :],
                                           s_buf.at[slot, g * topn + k], sem.at[slot])
                if wait:
                    cp.wait()
                else:
                    cp.start()


def _nsa_sample_attend_t_kernel(pt_ref, ids_ref, q_ref, oc_ref, gate_ref, ks_new_ref, kw_new_ref, win_ref, pool_ref,
                                o_ref, nw_ref, s_buf, sem):
    b = pl.program_id(0)
    slot = b % 2
    page = pool_ref.shape[2]
    per_page = page // SLC_BLOCK
    past = pt_ref.shape[1] * page
    n_past_blocks = past // SLC_BLOCK
    topn = ids_ref.shape[1] // KV_GROUPS
    n_keys = topn * page

    @pl.when(b == 0)
    def _():
        _page_tile_copies(pool_ref, pt_ref, ids_ref, s_buf, sem, b, slot, n_past_blocks, False)

    @pl.when(b + 1 < pl.num_programs(0))
    def _():
        _page_tile_copies(pool_ref, pt_ref, ids_ref, s_buf, sem, b + 1, 1 - slot, n_past_blocks, False)

    _page_tile_copies(pool_ref, pt_ref, ids_ref, s_buf, sem, b, slot, n_past_blocks, True)

    first_col = lax.broadcasted_iota(jnp.int32, (1, page), 1) == 0
    for g in range(KV_GROUPS):
        for k in range(topn):
            @pl.when(ids_ref[b, g * topn + k] >= n_past_blocks)
            def _(g=g, k=k):
                col = _row_to_col(ks_new_ref[0, :, g * LANES:(g + 1) * LANES])
                s_buf[slot, g * topn + k] = jnp.where(first_col, col, 0.0)

    q8 = q_ref[0]
    grp_of_row, slope = _head_consts()

    lane = lax.broadcasted_iota(jnp.int32, (1, n_keys), 1)
    tile_of_lane = lane // page
    within = lane % page
    s = jnp.zeros((NSA_HEADS, n_keys), F32)
    dist = jnp.zeros((NSA_HEADS, n_keys), F32)
    valid = jnp.zeros((NSA_HEADS, n_keys), F32)
    tiles = []
    for g in range(KV_GROUPS):
        kg = jnp.concatenate([s_buf[slot, g * topn + k] for k in range(topn)], axis=1)
        tiles.append(kg)
        pos = within
        half = jnp.zeros((1, n_keys), jnp.int32)
        for k in range(topn):
            j = ids_ref[b, g * topn + k]
            pos = pos + jnp.where(tile_of_lane == k, (j // per_page) * page, 0)
            half = jnp.where(tile_of_lane == k, j % per_page, half)
        dist_g = (past - pos).astype(F32)
        valid_g = jnp.where((within // SLC_BLOCK == half) & (dist_g >= 0.0), 1.0, 0.0)
        s = jnp.where(grp_of_row == g, _dot_split(q8, kg), s)
        dist = jnp.where(grp_of_row == g, dist_g, dist)
        valid = jnp.where(grp_of_row == g, valid_g, valid)
    p = _masked_softmax_rows(s, slope, dist, valid > 0.5)
    o_slc = jnp.zeros((NSA_HEADS, LANES), F32)
    for g in range(KV_GROUPS):
        o_slc = jnp.where(grp_of_row == g, _dot_split(p, tiles[g], nt=True), o_slc)

    wl = win_ref.shape[2]
    last_col = lax.broadcasted_iota(jnp.int32, (1, wl), 1) == wl - 1
    new_col = _row_to_col(kw_new_ref[0])
    nw_ref[0] = jnp.where(last_col, new_col, pltpu.roll(win_ref[0], wl - 1, 1))
    dist_w = (wl - 1 - lax.broadcasted_iota(jnp.int32, (1, wl), 1)).astype(F32)
    s = jnp.zeros((NSA_HEADS, wl), F32)
    for g in range(KV_GROUPS):
        s = jnp.where(grp_of_row == g, _dot_split(q8, nw_ref[0, g * LANES:(g + 1) * LANES, :]), s)
    p = _masked_softmax_rows(s, slope, dist_w, dist_w >= 0.0)
    o_win = jnp.zeros((NSA_HEADS, LANES), F32)
    for g in range(KV_GROUPS):
        o_win = jnp.where(grp_of_row == g, _dot_split(p, nw_ref[0, g * LANES:(g + 1) * LANES, :], nt=True), o_win)

    gates = jax.nn.sigmoid(gate_ref[0])
    o_ref[0] = gates[:, 0:1] * oc_ref[0] + gates[:, 1:2] * o_slc + gates[:, 2:3] * o_win


def _nsa_sample_attend_t(page_table, ids, q8, oc, gates8, ks_new, kw_new, win_t, pool_t):
    n_dec = page_table.shape[0]
    wl = win_t.shape[2]
    page = pool_t.shape[2]
    topn = ids.shape[1] // KV_GROUPS
    head = pl.BlockSpec((1, NSA_HEADS, LANES), lambda i, pt, sel: (i, 0, 0))
    new_row = pl.BlockSpec((1, 1, KV_COLS), lambda i, pt, sel: (i, 0, 0))
    win = pl.BlockSpec((1, KV_COLS, wl), lambda i, pt, sel: (i, 0, 0))
    grid_spec = pltpu.PrefetchScalarGridSpec(
        num_scalar_prefetch=2,
        grid=(n_dec,),
        in_specs=[head, head, head, new_row, new_row, win, pl.BlockSpec(memory_space=pl.ANY)],
        out_specs=[head, win],
        scratch_shapes=[pltpu.VMEM((2, KV_GROUPS * topn, LANES, page), F32),
                        pltpu.SemaphoreType.DMA((2,))],
    )
    return pl.pallas_call(
        _nsa_sample_attend_t_kernel,
        grid_spec=grid_spec,
        out_shape=[jax.ShapeDtypeStruct((n_dec, NSA_HEADS, LANES), F32),
                   jax.ShapeDtypeStruct((n_dec, KV_COLS, wl), F32)],
        compiler_params=_cparams("arbitrary"),
        name="nsa_sample_attend",
    )(page_table, ids, q8, oc, gates8, ks_new, kw_new, win_t, pool_t)


def _hgrn_sample_kernel(layer, hg_ref, lb_ref, g_ref, s_ref, o_ref, sn_ref):
    lb_all = _lower_bound(lb_ref, layer)
    hg = hg_ref[0]
    for h in range(HGRN_HEADS):
        part = lambda p, h=h: hg[p * HGRN_HEADS + h:p * HGRN_HEADS + h + 1]
        lb = lb_all[:, h * HGRN_DK:(h + 1) * HGRN_DK]
        fz = part(1)
        f = lb + (1.0 - lb) * jax.nn.sigmoid(fz)
        k = (1.0 - lb) * jax.nn.sigmoid(-fz)
        s_new = _row_to_col(f) * s_ref[0, h] + _row_to_col(k) * part(2)
        sn_ref[0, h] = s_new
        o = jnp.sum(_row_to_col(_silu(part(0))) * s_new, axis=0, keepdims=True)
        o_ref[0, h:h + 1, :] = _rms(o, g_ref[h:h + 1, :]) * _silu(part(3))


def _hgrn_sample(hg, lb_logits, g_out, state, layer):
    n_dec = hg.shape[0]
    st = pl.BlockSpec((1, HGRN_HEADS, HGRN_DK, HGRN_DV), lambda i: (i, 0, 0, 0))
    return pl.pallas_call(
        functools.partial(_hgrn_sample_kernel, layer),
        grid=(n_dec,),
        in_specs=[pl.BlockSpec((1, 4 * HGRN_HEADS, HGRN_DK), lambda i: (i, 0, 0)),
                  pl.BlockSpec(lb_logits.shape, lambda i: (0, 0)),
                  pl.BlockSpec((HGRN_HEADS, HGRN_DV), lambda i: (0, 0)),
                  st],
        out_specs=[pl.BlockSpec((1, HGRN_HEADS, HGRN_DV), lambda i: (i, 0, 0)), st],
        out_shape=[jax.ShapeDtypeStruct((n_dec, HGRN_HEADS, HGRN_DV), F32),
                   jax.ShapeDtypeStruct(state.shape, F32)],
        compiler_params=_cparams("parallel"),
        name="hgrn_sample",
    )(hg, lb_logits, g_out, state)


def _tile(n, pref):
    return pref if n % pref == 0 else n


def kernel(x_prompt, x_sample, c_prompt, c_sample, cache_cmp_kv, cache_slc_kv, cache_win_kv, state_hgrn, page_table, w_ada, b_ada, g_pre_mix, g_post_mix, g_pre_ffn, g_post_ffn, w_in, w_phi1, b_phi1, w_phi2, g_nsa_out, hgrn_lb_logits, g_hgrn_out, w_out, w_route_group, b_route_group, w_route_expert, b_route_expert, w_exp_gate, w_exp_up, w_exp_down):
    l = 0
    bp, t, d = x_prompt.shape
    n_dec = x_sample.shape[0]
    ada = _adaln(jnp.concatenate([c_prompt, c_sample], axis=0), w_ada[l], b_ada[l])
    sh_a, sc_a, gt_a, sh_f, sc_f, gt_f = [ada[:, None, j * d:(j + 1) * d] for j in range(6)]
    w_r, b_r = _prep_router(w_route_group[l], b_route_group[l], w_route_expert[l], b_route_expert[l])
    w01, b1r, w2n, w2t = _prep_compress(w_phi1[l], b_phi1[l], w_phi2[l])

    tile = 256
    w_nat, w_tr = _prep_w_in(w_in[l])
    (q, kv_c, kv_s, kv_w, ks_b, kw_b, hg, vs_t, vw_t, gates_t) = _inproj_prompt(
        x_prompt, sc_a[:bp], sh_a[:bp], g_pre_mix[l][None], w_nat.astype(BF16), w_tr.astype(BF16), tile)
    kc, kc_t = _compress_prompt(kv_c, w01, b1r, w2n, w2t)
    o_nsa = _nsa_prompt(q, kc, kc_t, ks_b, vs_t, kw_b, vw_t, gates_t, tile)
    o_h, s_p = _hgrn_prompt(hg, hgrn_lb_logits, g_hgrn_out[l], l, _tile(t, 512))
    x1, h2, comb = _postmix(o_nsa, o_h, x_prompt, gt_a[:bp], sc_f[:bp], sh_f[:bp], g_nsa_out[l], g_post_mix[l],
                            g_pre_ffn[l], w_out[l].astype(BF16), w_r, b_r, _tile(t, 512), False)
    wg, wu, wd = _prep_experts(w_exp_gate[l], w_exp_up[l], w_exp_down[l], BF16)
    y_p = _moe(h2, comb, x1, gt_f[:bp], g_post_ffn[l], wg, wu, wd, _tile(t, 1024), False)

    kv_shape = (1, bp, t, KV_GROUPS, 2, HEAD_DIM)
    w_keep = min(WINDOW, t)
    p_win = kv_w[:, t - w_keep:].reshape(1, bp, w_keep, KV_GROUPS, 2, HEAD_DIM)

    n_pool, page = cache_cmp_kv.shape[1], cache_cmp_kv.shape[2]
    wl = cache_win_kv.shape[2]
    past = page_table.shape[1] * page
    assert x_sample.shape[1] == 1 and wl == WINDOW and past % SLC_BLOCK == 0 and page % SLC_BLOCK == 0
    xs = x_sample.reshape(n_dec, d)
    proj = _inproj_sample(xs, sc_a[bp:, 0], sh_a[bp:, 0], g_pre_mix[l][None], w_in[l])
    sizes = [NSA_WIDTH, KV_COLS, KV_COLS, KV_COLS, 3 * NSA_HEADS, HGRN_WIDTH, HGRN_WIDTH, HGRN_WIDTH, HGRN_WIDTH]
    offs = np.cumsum([0] + sizes)
    seg = lambda j: proj[:, offs[j]:offs[j + 1]]
    pad_lanes = lambda a: jnp.pad(a, ((0, 0), (0, 0), (0, LANES - a.shape[2])))
    q8 = pad_lanes(seg(0).reshape(n_dec, NSA_HEADS, HEAD_DIM) * (HEAD_DIM ** -0.5))
    gates8 = pad_lanes(seg(4).reshape(n_dec, NSA_HEADS, 3))
    hg_s = proj[:, offs[5]:offs[9]].reshape(n_dec, 4 * HGRN_HEADS, HGRN_DK)
    tiles = lambda c: jnp.swapaxes(c.reshape(c.shape[0], c.shape[1], KV_COLS), 1, 2)
    oc, ids = _nsa_sample_select(page_table, q8, tiles(cache_cmp_kv[l]), w01, b1r, w2n, w2t)
    o8, s_win_t = _nsa_sample_attend_t(page_table, ids[..., 0].reshape(n_dec, -1), q8, oc, gates8, seg(2)[:, None],
                                       seg(3)[:, None], tiles(cache_win_kv[l]), tiles(cache_slc_kv[l]))
    s_win = jnp.swapaxes(s_win_t, 1, 2)
    o_nsa_s = o8[:, :, HEAD_DIM:].reshape(1, n_dec, NSA_WIDTH)
    oh_s, s_s = _hgrn_sample(hg_s, hgrn_lb_logits, g_hgrn_out[l], state_hgrn[l], l)
    as_row = lambda a: a[bp:].reshape(1, n_dec, d)
    x1s, h2s, comb_s = _postmix(o_nsa_s, oh_s.reshape(1, n_dec, HGRN_WIDTH), xs[None], as_row(gt_a), as_row(sc_f),
                                as_row(sh_f), g_nsa_out[l], g_post_mix[l], g_pre_ffn[l], w_out[l], w_r, b_r,
                                n_dec, True)
    wg32, wu32, wd32 = _prep_experts(w_exp_gate[l], w_exp_up[l], w_exp_down[l], F32)
    y_s = _moe(h2s, comb_s, x1s, as_row(gt_f), g_post_ffn[l], wg32, wu32, wd32, n_dec, True)

    new_shape = (1, n_dec, 1, KV_GROUPS, 2, HEAD_DIM)
    return (y_p, y_s.reshape(n_dec, 1, d), kv_c.reshape(kv_shape), kv_s.reshape(kv_shape), p_win, s_p[None],
            seg(1).reshape(new_shape), seg(2).reshape(new_shape),
            s_win.reshape(1, n_dec, wl, KV_GROUPS, 2, HEAD_DIM), s_s[None])
```
